```python
import math
import numpy as np
import jax, jax.numpy as jnp
from jax import lax

D_MODEL = 1024
BATCH = 8
SEQ = 8192
DEPTH = 1

NSA_HEADS = 16
NSA_KV_HEADS = 4
NSA_GROUP = NSA_HEADS // NSA_KV_HEADS
HEAD_DIM = 64
CMP_LEN = 32
CMP_STRIDE = 16
CMP_HIDDEN = 2 * HEAD_DIM
SEL_BLOCK = 64
SEL_TOPK = 16
WINDOW = 512
NSA_QBLOCK = 32
NSA_WIDTH = NSA_HEADS * HEAD_DIM
KV_WIDTH = NSA_KV_HEADS * HEAD_DIM
BIG = 1e4
NEG = -1e30

SSM_EXPAND = 2
SSM_INNER = SSM_EXPAND * D_MODEL
SSM_HEAD_DIM = 64
SSM_HEADS = SSM_INNER // SSM_HEAD_DIM
SSM_GROUPS = 4
SSM_HPG = SSM_HEADS // SSM_GROUPS
SSM_STATE = 128
CONV_WIDTH = 4
SSM_CHUNK = 256
CONV_DIM = SSM_INNER + 2 * SSM_GROUPS * SSM_STATE

FFN_HIDDEN = 2816
EPS = 1e-6

PROJ_SIZES = [NSA_WIDTH,
              KV_WIDTH, KV_WIDTH,
              KV_WIDTH, KV_WIDTH,
              KV_WIDTH, KV_WIDTH,
              NSA_HEADS * 3,
              SSM_INNER,
              CONV_DIM,
              SSM_HEADS,
              2 * D_MODEL]
PROJ_WIDTH = sum(PROJ_SIZES)
SPLIT_POINTS = [int(v) for v in np.cumsum(PROJ_SIZES)[:-1]]

kernel_name = "hybrid_nsa_mamba2_macaron_block"


def rmsnorm(x, g):
    xf = x.astype(jnp.float32)
    y = xf * lax.rsqrt(jnp.mean(xf * xf, axis=-1, keepdims=True) + EPS)
    return (y * g.astype(jnp.float32)).astype(x.dtype)


def swiglu(x, w_gate, w_up, w_down):
    return (jax.nn.silu(x @ w_gate) * (x @ w_up)) @ w_down


def alibi_slopes(n):
    return np.array([2.0 ** (-8.0 * (h + 1) / n) for h in range(n)], dtype=np.float32)


def masked_softmax(s, mask, axis=-1):
    p = jax.nn.softmax(jnp.where(mask, s, NEG), axis=axis)
    return jnp.where(mask, p, 0.0)


def compress_kv(k, pos, w1, w2):
    b, s = k.shape[:2]
    nc = (s - CMP_LEN) // CMP_STRIDE + 1
    idx = np.arange(nc)[:, None] * CMP_STRIDE + np.arange(CMP_LEN)[None, :]
    blk = k[:, idx] + pos[:, None, :]
    blk = jnp.moveaxis(blk, 3, 2).reshape(b, nc, NSA_KV_HEADS, CMP_LEN * HEAD_DIM)
    out = jax.nn.silu(blk @ w1) @ w2
    return jnp.transpose(out, (0, 2, 1, 3))


def nsa_attention(q, k_cmp, v_cmp, k_slc, v_slc, k_win, v_win, gates, cmp_pos, w_ck1, w_ck2, w_cv1, w_cv2):
    b, s = q.shape[:2]
    dt = q.dtype
    f32 = jnp.float32
    scale = HEAD_DIM ** -0.5
    slopes = jnp.asarray(alibi_slopes(NSA_HEADS).reshape(NSA_KV_HEADS, NSA_GROUP))

    ck = compress_kv(k_cmp, cmp_pos, w_ck1, w_ck2)
    cv = compress_kv(v_cmp, cmp_pos, w_cv1, w_cv2)
    nc = ck.shape[2]
    c_start = np.arange(nc) * CMP_STRIDE
    c_end = jnp.asarray(c_start + CMP_LEN - 1)
    ns = s // SEL_BLOCK
    s_start = np.arange(ns) * SEL_BLOCK
    overlap = jnp.asarray(((c_start[:, None] <= s_start[None, :] + SEL_BLOCK - 1) &
                           (c_start[:, None] + CMP_LEN - 1 >= s_start[None, :])).astype(np.float32))
    n_sel = min(SEL_TOPK, ns)

    ks = jnp.transpose(k_slc, (0, 2, 1, 3)).reshape(b, NSA_KV_HEADS, ns, SEL_BLOCK, HEAD_DIM)
    vs = jnp.transpose(v_slc, (0, 2, 1, 3)).reshape(b, NSA_KV_HEADS, ns, SEL_BLOCK, HEAD_DIM)
    pad_w = ((0, 0), (0, 0), (WINDOW, 0), (0, 0))
    kw = jnp.pad(jnp.transpose(k_win, (0, 2, 1, 3)), pad_w)
    vw = jnp.pad(jnp.transpose(v_win, (0, 2, 1, 3)), pad_w)
    qh = q.reshape(b, s, NSA_KV_HEADS, NSA_GROUP, HEAD_DIM).transpose(0, 2, 3, 1, 4)
    gh = gates.reshape(b, s, NSA_KV_HEADS, NSA_GROUP, 3).transpose(0, 2, 3, 1, 4)
    b_idx = jnp.arange(b)[:, None, None, None]
    g_idx = jnp.arange(NSA_KV_HEADS)[None, :, None, None]
    j = jnp.arange(ns)
    nb = s // NSA_QBLOCK

    def block(i):
        q0 = i * NSA_QBLOCK
        qb = lax.dynamic_slice_in_dim(qh, q0, NSA_QBLOCK, axis=3)
        gb = jax.nn.sigmoid(lax.dynamic_slice_in_dim(gh, q0, NSA_QBLOCK, axis=3).astype(f32))
        t = q0 + jnp.arange(NSA_QBLOCK)

        dist_c = t[:, None] - c_end[None, :]
        s_c = jnp.einsum('bghqd,bgcd->bghqc', qb, ck, preferred_element_type=f32) * scale
        s_c = s_c - slopes[:, :, None, None] * dist_c.astype(f32)
        p_c = masked_softmax(s_c, dist_c >= 0)
        o_c = jnp.einsum('bghqc,bgcd->bghqd', p_c.astype(dt), cv)

        imp = jnp.einsum('bghqc,cj->bgqj', p_c, overlap)
        cur = (t // SEL_BLOCK)[:, None]
        forced = (j == 0) | (j == cur) | (j == cur - 1)
        imp = jnp.where(forced, BIG, imp)
        imp = jnp.where(j <= cur, imp, -BIG)
        _, sel = lax.top_k(imp, n_sel)
        sel_valid = sel <= cur
        k_sel = ks[b_idx, g_idx, sel]
        v_sel = vs[b_idx, g_idx, sel]
        pos_s = sel[..., None] * SEL_BLOCK + jnp.arange(SEL_BLOCK)
        dist_s = t[:, None, None] - pos_s
        mask_s = (dist_s >= 0) & sel_valid[..., None]
        s_s = jnp.einsum('bghqd,bgqnkd->bghqnk', qb, k_sel, preferred_element_type=f32) * scale
        s_s = s_s - slopes[:, :, None, None, None] * dist_s[:, :, None].astype(f32)
        p_s = masked_softmax(s_s, mask_s[:, :, None], axis=(-2, -1))
        o_s = jnp.einsum('bghqnk,bgqnkd->bghqd', p_s.astype(dt), v_sel)

        kwb = lax.dynamic_slice_in_dim(kw, q0, WINDOW + NSA_QBLOCK, axis=2)
        vwb = lax.dynamic_slice_in_dim(vw, q0, WINDOW + NSA_QBLOCK, axis=2)
        pos_w = q0 - WINDOW + jnp.arange(WINDOW + NSA_QBLOCK)
        dist_w = t[:, None] - pos_w[None, :]
        mask_w = (dist_w >= 0) & (dist_w < WINDOW) & (pos_w[None, :] >= 0)
        s_w = jnp.einsum('bghqd,bgkd->bghqk', qb, kwb, preferred_element_type=f32) * scale
        s_w = s_w - slopes[:, :, None, None] * dist_w.astype(f32)
        p_w = masked_softmax(s_w, mask_w)
        o_w = jnp.einsum('bghqk,bgkd->bghqd', p_w.astype(dt), vwb)

        o = gb[..., 0:1] * o_c + gb[..., 1:2] * o_s + gb[..., 2:3] * o_w
        return o.astype(dt)

    out = lax.map(block, jnp.arange(nb))
    return out.transpose(1, 0, 4, 2, 3, 5).reshape(b, s, NSA_WIDTH)


def causal_depthwise_conv(x, w, bias):
    c = x.shape[-1]
    y = lax.conv_general_dilated(x, w[:, None, :].astype(x.dtype), window_strides=(1,),
                                 padding=[(CONV_WIDTH - 1, 0)],
                                 dimension_numbers=('NWC', 'WIO', 'NWC'),
                                 feature_group_count=c)
    return y + bias.astype(x.dtype)


def ssd_chunked_scan(x, dt, a, bm, cm):
    bsz, s = x.shape[:2]
    pad = (-s) % SSM_CHUNK
    def padt(v):
        return jnp.pad(v, [(0, 0), (0, pad)] + [(0, 0)] * (v.ndim - 2))
    x, dt, bm, cm = padt(x), padt(dt), padt(bm), padt(cm)
    nc = (s + pad) // SSM_CHUNK
    def chunk(v):
        return jnp.moveaxis(v.reshape((bsz, nc, SSM_CHUNK) + v.shape[2:]), 1, 0)
    tri = jnp.tril(jnp.ones((SSM_CHUNK, SSM_CHUNK), dtype=bool))[None, :, :, None, None]

    def step(state, inp):
        xc, dtc, bc, cc = inp
        cum = jnp.cumsum(dtc * a, axis=1)
        seg = cum[:, :, None] - cum[:, None, :]
        decay = jnp.exp(jnp.where(tri, seg, -jnp.inf))
        cb = jnp.einsum('btgn,bsgn->btsg', cc, bc)
        w = cb[..., None] * decay * dtc[:, None]
        y = jnp.einsum('btsgh,bsghp->btghp', w, xc)
        y = y + jnp.einsum('btgn,bghpn->btghp', cc, state) * jnp.exp(cum)[..., None]
        to_end = jnp.exp(cum[:, -1:] - cum) * dtc
        state = state * jnp.exp(cum[:, -1])[..., None, None] + \
            jnp.einsum('bsgh,bsgn,bsghp->bghpn', to_end, bc, xc)
        return state, y

    state0 = jnp.zeros((bsz, SSM_GROUPS, SSM_HPG, SSM_HEAD_DIM, SSM_STATE), jnp.float32)
    _, ys = lax.scan(step, state0, (chunk(x), chunk(dt), chunk(bm), chunk(cm)))
    return jnp.moveaxis(ys, 0, 1).reshape(bsz, nc * SSM_CHUNK, SSM_GROUPS, SSM_HPG, SSM_HEAD_DIM)[:, :s]


def mamba2_mixer(z, xbc, dt_raw, conv_w, conv_b, dt_bias, a_log, d_skip, ssm_norm):
    b, s = z.shape[:2]
    f32 = jnp.float32
    xbc = jax.nn.silu(causal_depthwise_conv(xbc, conv_w, conv_b)).astype(f32)
    xs, bm, cm = jnp.split(xbc, [SSM_INNER, SSM_INNER + SSM_GROUPS * SSM_STATE], axis=-1)
    xs = xs.reshape(b, s, SSM_GROUPS, SSM_HPG, SSM_HEAD_DIM)
    bm = bm.reshape(b, s, SSM_GROUPS, SSM_STATE)
    cm = cm.reshape(b, s, SSM_GROUPS, SSM_STATE)
    dt = jax.nn.softplus(dt_raw.astype(f32) + dt_bias.astype(f32)).reshape(b, s, SSM_GROUPS, SSM_HPG)
    a = -jnp.exp(a_log.astype(f32)).reshape(SSM_GROUPS, SSM_HPG)
    y = ssd_chunked_scan(xs, dt, a, bm, cm)
    y = y + d_skip.astype(f32).reshape(SSM_GROUPS, SSM_HPG)[..., None] * xs
    y = y.reshape(b, s, SSM_INNER) * jax.nn.silu(z.astype(f32))
    yg = y.reshape(b, s, SSM_GROUPS, SSM_INNER // SSM_GROUPS)
    yg = yg * lax.rsqrt(jnp.mean(yg * yg, axis=-1, keepdims=True) + EPS)
    return (yg.reshape(b, s, SSM_INNER) * ssm_norm.astype(f32)).astype(z.dtype)


def hybrid_mixer(u, w_in, cmp_pos, w_ck1, w_ck2, w_cv1, w_cv2, conv_w, conv_b, dt_bias, a_log,
                 d_skip, ssm_norm, w_attn_branch, w_ssm_branch, w_out):
    b, s, _ = u.shape
    dt = u.dtype
    proj = u @ w_in
    q, kc, vc, ks, vs, kw, vw, ng, z, xbc, dtr, mg = jnp.split(proj, SPLIT_POINTS, axis=-1)
    kv = lambda v: v.reshape(b, s, NSA_KV_HEADS, HEAD_DIM)
    y_a = nsa_attention(q.reshape(b, s, NSA_HEADS, HEAD_DIM), kv(kc), kv(vc), kv(ks), kv(vs),
                        kv(kw), kv(vw), ng.reshape(b, s, NSA_HEADS, 3),
                        cmp_pos, w_ck1, w_ck2, w_cv1, w_cv2)
    y_m = mamba2_mixer(z, xbc, dtr, conv_w, conv_b, dt_bias, a_log, d_skip, ssm_norm)
    gates = jax.nn.sigmoid(mg.astype(jnp.float32)).reshape(b, s, 2, D_MODEL)
    merged = gates[:, :, 0] * (y_a @ w_attn_branch) + gates[:, :, 1] * (y_m @ w_ssm_branch)
    return merged.astype(dt) @ w_out


def setup_inputs(seed: int = 0) -> dict:
    key = jax.random.key(seed)
    keys = jax.random.split(key, 32)
    cnt = [0]
    def nk():
        k = keys[cnt[0]]
        cnt[0] += 1
        return k
    def nrm(shape, scale):
        return jax.random.normal(nk(), shape, jnp.float32) * scale
    def gain(shape):
        return 1.0 + nrm(shape, 0.02)
    L = DEPTH
    D = D_MODEL
    x = nrm((BATCH, SEQ, D), 1.0)
    ffn1_pre = gain((L, D))
    ffn1_gate = nrm((L, D, FFN_HIDDEN), D ** -0.5)
    ffn1_up = nrm((L, D, FFN_HIDDEN), D ** -0.5)
    ffn1_down = nrm((L, FFN_HIDDEN, D), FFN_HIDDEN ** -0.5)
    ffn1_post = gain((L, D))
    mix_pre = gain((L, D))
    w_in = nrm((L, D, PROJ_WIDTH), D ** -0.5)
    cmp_pos = nrm((L, CMP_LEN, HEAD_DIM), 0.1)
    w_ck1 = nrm((L, CMP_LEN * HEAD_DIM, CMP_HIDDEN), (CMP_LEN * HEAD_DIM) ** -0.5)
    w_ck2 = nrm((L, CMP_HIDDEN, HEAD_DIM), CMP_HIDDEN ** -0.5)
    w_cv1 = nrm((L, CMP_LEN * HEAD_DIM, CMP_HIDDEN), (CMP_LEN * HEAD_DIM) ** -0.5)
    w_cv2 = nrm((L, CMP_HIDDEN, HEAD_DIM), CMP_HIDDEN ** -0.5)
    conv_w = nrm((L, CONV_WIDTH, CONV_DIM), CONV_WIDTH ** -0.5)
    conv_b = nrm((L, CONV_DIM), 0.01)
    dt0 = jnp.exp(jax.random.uniform(nk(), (L, SSM_HEADS), jnp.float32,
                                     minval=math.log(1e-3), maxval=math.log(1e-1)))
    dt_bias = dt0 + jnp.log(-jnp.expm1(-dt0))
    a_log = jnp.log(jax.random.uniform(nk(), (L, SSM_HEADS), jnp.float32, minval=1.0, maxval=16.0))
    d_skip = gain((L, SSM_HEADS))
    ssm_norm = gain((L, SSM_INNER))
    w_attn_branch = nrm((L, NSA_WIDTH, D), NSA_WIDTH ** -0.5)
    w_ssm_branch = nrm((L, SSM_INNER, D), SSM_INNER ** -0.5)
    w_out = nrm((L, D, D), D ** -0.5)
    mix_post = gain((L, D))
    ffn2_pre = gain((L, D))
    ffn2_gate = nrm((L, D, FFN_HIDDEN), D ** -0.5)
    ffn2_up = nrm((L, D, FFN_HIDDEN), D ** -0.5)
    ffn2_down = nrm((L, FFN_HIDDEN, D), FFN_HIDDEN ** -0.5)
    ffn2_post = gain((L, D))
    return {"x": x, "ffn1_pre": ffn1_pre, "ffn1_gate": ffn1_gate, "ffn1_up": ffn1_up,
            "ffn1_down": ffn1_down, "ffn1_post": ffn1_post, "mix_pre": mix_pre, "w_in": w_in,
            "cmp_pos": cmp_pos, "w_ck1": w_ck1, "w_ck2": w_ck2, "w_cv1": w_cv1, "w_cv2": w_cv2,
            "conv_w": conv_w, "conv_b": conv_b, "dt_bias": dt_bias, "a_log": a_log,
            "d_skip": d_skip, "ssm_norm": ssm_norm, "w_attn_branch": w_attn_branch,
            "w_ssm_branch": w_ssm_branch, "w_out": w_out, "mix_post": mix_post,
            "ffn2_pre": ffn2_pre, "ffn2_gate": ffn2_gate, "ffn2_up": ffn2_up,
            "ffn2_down": ffn2_down, "ffn2_post": ffn2_post}


def reference(x, ffn1_pre, ffn1_gate, ffn1_up, ffn1_down, ffn1_post, mix_pre, w_in, cmp_pos,
              w_ck1, w_ck2, w_cv1, w_cv2, conv_w, conv_b, dt_bias, a_log, d_skip, ssm_norm,
              w_attn_branch, w_ssm_branch, w_out, mix_post, ffn2_pre, ffn2_gate, ffn2_up,
              ffn2_down, ffn2_post):
    h = x
    for l in range(DEPTH):
        f = swiglu(rmsnorm(h, ffn1_pre[l]), ffn1_gate[l], ffn1_up[l], ffn1_down[l])
        h = h + 0.5 * rmsnorm(f, ffn1_post[l])
        m = hybrid_mixer(rmsnorm(h, mix_pre[l]), w_in[l], cmp_pos[l], w_ck1[l], w_ck2[l],
                         w_cv1[l], w_cv2[l], conv_w[l], conv_b[l], dt_bias[l], a_log[l],
                         d_skip[l], ssm_norm[l], w_attn_branch[l], w_ssm_branch[l], w_out[l])
        h = h + rmsnorm(m, mix_post[l])
        f = swiglu(rmsnorm(h, ffn2_pre[l]), ffn2_gate[l], ffn2_up[l], ffn2_down[l])
        h = h + 0.5 * rmsnorm(f, ffn2_post[l])
    return h
```

```python
import functools

import numpy as np
import jax
import jax.numpy as jnp
from jax import lax
from jax.experimental import pallas as pl
from jax.experimental.pallas import tpu as pltpu

F32 = jnp.float32
BF16 = jnp.bfloat16

D_MODEL = 1024
NSA_HEADS = 16
NSA_KV_HEADS = 4
NSA_GROUP = NSA_HEADS // NSA_KV_HEADS
HEAD_DIM = 64
CMP_LEN = 32
CMP_STRIDE = 16
CMP_HIDDEN = 2 * HEAD_DIM
SEL_BLOCK = 64
SEL_TOPK = 16
WINDOW = 512
NSA_WIDTH = NSA_HEADS * HEAD_DIM
KV_WIDTH = NSA_KV_HEADS * HEAD_DIM
BIG = 1e4
NEG = -1e30

SSM_INNER = 2 * D_MODEL
SSM_HEAD_DIM = 64
SSM_HEADS = SSM_INNER // SSM_HEAD_DIM
SSM_GROUPS = 4
SSM_HPG = SSM_HEADS // SSM_GROUPS
SSM_STATE = 128
CONV_WIDTH = 4
SSM_CHUNK = 256
CONV_DIM = SSM_INNER + 2 * SSM_GROUPS * SSM_STATE
FFN_HIDDEN = 2816
EPS = 1e-6

LANES = 128
VMEM_LIMIT = 56 * 1024 * 1024


def _cparams(sem):
    return pltpu.CompilerParams(dimension_semantics=sem, vmem_limit_bytes=VMEM_LIMIT)


def _rms(x, g):
    return x * lax.rsqrt(jnp.mean(x * x, axis=-1, keepdims=True) + EPS) * g


def _split3(x):
    hi = x.astype(BF16)
    r1 = x - hi.astype(F32)
    mid = r1.astype(BF16)
    lo = (r1 - mid.astype(F32)).astype(BF16)
    return hi, mid, lo


def _dot01_l(a01, x):
    hi, mid, lo = _split3(x)
    d = lambda v: jnp.dot(a01, v, preferred_element_type=F32)
    return d(hi) + (d(mid) + d(lo))


def _dot01_r(x, a01):
    hi, mid, lo = _split3(x)
    d = lambda v: jnp.dot(v, a01, preferred_element_type=F32)
    return d(hi) + (d(mid) + d(lo))


FFN_TM = 256
FFN_HC = 256


def _ffn_kernel(x_ref, pre_ref, wg_ref, wu_ref, wd_ref, post_ref, o_ref):
    x = x_ref[...]
    ub = _rms(x, pre_ref[...]).astype(BF16)
    acc = jnp.zeros(x.shape, F32)
    for c in range(FFN_HIDDEN // FFN_HC):
        sl = slice(c * FFN_HC, (c + 1) * FFN_HC)
        g = jnp.dot(ub, wg_ref[:, sl], preferred_element_type=F32)
        u = jnp.dot(ub, wu_ref[:, sl], preferred_element_type=F32)
        a = (g * jax.nn.sigmoid(g)) * u
        acc = acc + jnp.dot(a.astype(BF16), wd_ref[sl, :], preferred_element_type=F32)
    o_ref[...] = x + 0.5 * _rms(acc, post_ref[...])


def _ffn(h2d, pre, wg, wu, wd, post):
    t, d = h2d.shape
    tm = min(FFN_TM, t)
    const = lambda shape: pl.BlockSpec(shape, lambda i: (0, 0), pipeline_mode=pl.Buffered(1))
    return pl.pallas_call(
        _ffn_kernel,
        grid=(t // tm,),
        in_specs=[pl.BlockSpec((tm, d), lambda i: (i, 0)),
                  const((1, d)), const((d, FFN_HIDDEN)), const((d, FFN_HIDDEN)),
                  const((FFN_HIDDEN, d)), const((1, d))],
        out_specs=pl.BlockSpec((tm, d), lambda i: (i, 0)),
        out_shape=jax.ShapeDtypeStruct((t, d), F32),
        compiler_params=_cparams(("parallel",)),
        name="ffn",
    )(h2d, pre.reshape(1, d), wg.astype(BF16), wu.astype(BF16), wd.astype(BF16), post.reshape(1, d))


PROJ_TM = 1024
PROJ_TN = 512


def _norm_proj_kernel(x_ref, g_ref, w_ref, o_ref, u_ref, *, scale):
    @pl.when(pl.program_id(1) == 0)
    def _():
        u_ref[...] = _rms(x_ref[...], g_ref[...]).astype(BF16)

    acc = jnp.dot(u_ref[...], w_ref[...], preferred_element_type=F32)
    if scale != 1.0:
        acc = acc * scale
    o_ref[...] = acc.astype(o_ref.dtype)


def _norm_proj(h2d, g, w, out_dtype, scale=1.0):
    t, d = h2d.shape
    n = w.shape[1]
    tm = min(PROJ_TM, t)
    tn = min(PROJ_TN, n)
    return pl.pallas_call(
        functools.partial(_norm_proj_kernel, scale=scale),
        grid=(t // tm, n // tn),
        in_specs=[pl.BlockSpec((tm, d), lambda i, j: (i, 0)),
                  pl.BlockSpec((1, d), lambda i, j: (0, 0)),
                  pl.BlockSpec((d, tn), lambda i, j: (0, j))],
        out_specs=pl.BlockSpec((tm, tn), lambda i, j: (i, j)),
        out_shape=jax.ShapeDtypeStruct((t, n), out_dtype),
        scratch_shapes=[pltpu.VMEM((tm, d), BF16)],
        compiler_params=_cparams(("parallel", "arbitrary")),
        name="norm_proj",
    )(h2d, g.reshape(1, d), w.astype(BF16))


def _compress_kernel(r_ref, pt_ref, pb_ref, w1t_ref, w1b_ref, w2_ref, o_ref):
    r = r_ref[0].astype(F32)
    top = jnp.dot((r + pt_ref[...]).astype(BF16), w1t_ref[...], preferred_element_type=F32)
    bot = jnp.dot((r + pb_ref[...]).astype(BF16), w1b_ref[...], preferred_element_type=F32)
    ncp = r.shape[0]
    h = top + pltpu.roll(bot, ncp - 1, 0)
    a = (h * jax.nn.sigmoid(h)).astype(BF16)
    o_ref[0] = jnp.dot(a, w2_ref[...], preferred_element_type=F32).astype(o_ref.dtype)


def _compress(k, pos, w1, w2):
    b, s, _ = k.shape
    ncp = s // CMP_STRIDE
    half = CMP_LEN // 2
    rw = half * KV_WIDTH
    eye = jnp.eye(NSA_KV_HEADS, dtype=F32)
    w1r = w1.reshape(CMP_LEN, HEAD_DIM, CMP_HIDDEN)

    def big(wpart):
        return jnp.einsum('ldj,gk->lgdkj', wpart, eye).reshape(rw, NSA_KV_HEADS * CMP_HIDDEN).astype(BF16)

    def posrow(p):
        return jnp.broadcast_to(p[:, None, :], (half, NSA_KV_HEADS, HEAD_DIM)).reshape(1, rw)

    w2big = jnp.einsum('jd,gk->gjkd', w2, eye).reshape(NSA_KV_HEADS * CMP_HIDDEN, KV_WIDTH).astype(BF16)
    nh = NSA_KV_HEADS * CMP_HIDDEN
    const = lambda shape: pl.BlockSpec(shape, lambda i: (0, 0))
    return pl.pallas_call(
        _compress_kernel,
        grid=(b,),
        in_specs=[pl.BlockSpec((1, ncp, rw), lambda i: (i, 0, 0)),
                  const((1, rw)), const((1, rw)), const((rw, nh)), const((rw, nh)),
                  const((nh, KV_WIDTH))],
        out_specs=pl.BlockSpec((1, ncp, KV_WIDTH), lambda i: (i, 0, 0)),
        out_shape=jax.ShapeDtypeStruct((b, ncp, KV_WIDTH), BF16),
        compiler_params=_cparams(("parallel",)),
        name="compress",
    )(k.reshape(b, ncp, rw), posrow(pos[:half]), posrow(pos[half:]), big(w1r[:half]), big(w1r[half:]), w2big)


ATT_TQ = 128
WIN_TK = 128


def _online_update(s, valid, m, l, acc, vT):
    s = jnp.where(valid, s, NEG)
    m_new = jnp.maximum(m, jnp.max(s, axis=0, keepdims=True))
    alpha = jnp.exp(m - m_new)
    p = jnp.where(valid, jnp.exp(s - m_new), 0.0)
    l = l * alpha + jnp.sum(p, axis=0, keepdims=True)
    acc = acc * alpha + jnp.dot(vT, p.astype(BF16), preferred_element_type=F32)
    return m_new, l, acc


def _attn1_kernel(slopes_ref, qT_ref, ck_ref, cvT_ref, kw_ref, vwT_ref, gates_ref, ovT_ref,
                  part_ref, sel_ref, cnt_ref, *, tq, ncp, ns, n_sel):
    g = pl.program_id(1)
    q0 = pl.program_id(2) * tq
    t_row = q0 + lax.broadcasted_iota(jnp.int32, (1, tq), 1)

    c_end = lax.broadcasted_iota(jnp.int32, (ncp, 1), 0) * CMP_STRIDE + (CMP_LEN - 1)
    dist_c = t_row - c_end
    valid_c = dist_c >= 0
    dist_cf = dist_c.astype(F32)
    ck = ck_ref[0, 0]
    cvT = cvT_ref[0, 0]
    psum = jnp.zeros((ncp, tq), F32)
    o_c = []
    for h in range(NSA_GROUP):
        slope = slopes_ref[g * NSA_GROUP + h]
        s = jnp.dot(ck, qT_ref[0, 0, h], preferred_element_type=F32) - slope * dist_cf
        s = jnp.where(valid_c, s, NEG)
        e = jnp.exp(s - jnp.max(s, axis=0, keepdims=True))
        p = jnp.where(valid_c, e * (1.0 / jnp.sum(e, axis=0, keepdims=True)), 0.0)
        psum = psum + p
        o_c.append(jnp.dot(cvT, p.astype(BF16), preferred_element_type=F32))

    imp = _dot01_l(ovT_ref[...], psum)
    j_col = lax.broadcasted_iota(jnp.int32, (ns, 1), 0)
    j_colf = j_col.astype(F32)
    cur = jnp.right_shift(t_row, SEL_BLOCK.bit_length() - 1)
    forced = (j_col == 0) | (j_col == cur) | (j_col == cur - 1)
    imp = jnp.where(forced, BIG, imp)
    work = jnp.where(j_col <= cur, imp, -BIG)
    selm = jnp.zeros((ns, tq), F32)
    for _ in range(n_sel):
        mx = jnp.max(work, axis=0, keepdims=True)
        idx = jnp.min(jnp.where(work == mx, j_colf, float(ns)), axis=0, keepdims=True)
        pick = j_colf == idx
        selm = jnp.where(pick, 1.0, selm)
        work = jnp.where(pick, -jnp.inf, work)
    selm = jnp.where(j_col <= cur, selm, 0.0)
    sel_ref[0, 0] = selm
    cnt_ref[0, 0, 0] = lax.dot_general(jnp.ones((8, tq), BF16), selm.astype(BF16),
                                       (((1,), (1,)), ((), ())), preferred_element_type=F32)

    n_kt = (WINDOW + tq) // WIN_TK
    kt_lo = jnp.maximum(0, (WINDOW - q0) // WIN_TK)

    def body(kt, carry):
        k0 = pl.multiple_of(q0 - WINDOW + kt * WIN_TK, WIN_TK)
        kk = kw_ref[0, 0, pl.ds(k0, WIN_TK), :]
        vT = vwT_ref[0, 0, :, pl.ds(k0, WIN_TK)]
        dist = t_row - (k0 + lax.broadcasted_iota(jnp.int32, (WIN_TK, 1), 0))
        valid = (dist >= 0) & (dist < WINDOW)
        distf = dist.astype(F32)
        out = []
        for h in range(NSA_GROUP):
            m, l, acc = carry[h]
            slope = slopes_ref[g * NSA_GROUP + h]
            s = jnp.dot(kk, qT_ref[0, 0, h], preferred_element_type=F32) - slope * distf
            out.append(_online_update(s, valid, m, l, acc, vT))
        return tuple(out)

    init = tuple((jnp.full((1, tq), NEG, F32), jnp.zeros((1, tq), F32), jnp.zeros((HEAD_DIM, tq), F32))
                 for _ in range(NSA_GROUP))
    win = lax.fori_loop(kt_lo, n_kt, body, init)

    gsig = jax.nn.sigmoid(gates_ref[0, 0])
    for h in range(NSA_GROUP):
        _, l, acc = win[h]
        part_ref[0, 0, h] = gsig[0, h:h + 1, :] * o_c[h] + gsig[2, h:h + 1, :] * (acc * (1.0 / l))


def _attn1(slopes, qT, ck, cvT, kw, vwT, gatesT, ovT):
    b, kv, grp, dh, s = qT.shape
    tq = min(ATT_TQ, s)
    nqt = s // tq
    ncp = ck.shape[2]
    ns = s // SEL_BLOCK
    n_sel = min(SEL_TOPK, ns)
    kern = functools.partial(_attn1_kernel, tq=tq, ncp=ncp, ns=ns, n_sel=n_sel)
    return pl.pallas_call(
        kern,
        grid=(b, kv, nqt),
        in_specs=[pl.BlockSpec(memory_space=pltpu.SMEM),
                  pl.BlockSpec((1, 1, grp, dh, tq), lambda bi, g, i: (bi, g, 0, 0, i)),
                  pl.BlockSpec((1, 1, ncp, dh), lambda bi, g, i: (bi, g, 0, 0)),
                  pl.BlockSpec((1, 1, dh, ncp), lambda bi, g, i: (bi, g, 0, 0)),
                  pl.BlockSpec((1, 1, s, dh), lambda bi, g, i: (bi, g, 0, 0)),
                  pl.BlockSpec((1, 1, dh, s), lambda bi, g, i: (bi, g, 0, 0)),
                  pl.BlockSpec((1, 1, 3, grp, tq), lambda bi, g, i: (bi, g, 0, 0, i)),
                  pl.BlockSpec((ns, ncp), lambda bi, g, i: (0, 0))],
        out_specs=[pl.BlockSpec((1, 1, grp, dh, tq), lambda bi, g, i: (bi, g, 0, 0, i)),
                   pl.BlockSpec((1, 1, ns, tq), lambda bi, g, i: (bi, g, 0, i)),
                   pl.BlockSpec((1, 1, 1, 8, ns), lambda bi, g, i: (bi, g, i, 0, 0))],
        out_shape=[jax.ShapeDtypeStruct((b, kv, grp, dh, s), F32),
                   jax.ShapeDtypeStruct((b, kv, ns, s), F32),
                   jax.ShapeDtypeStruct((b, kv, nqt, 8, ns), F32)],
        compiler_params=_cparams(("parallel", "parallel", "arbitrary")),
        name="attn1",
    )(slopes, qT, ck, cvT, kw, vwT, gatesT, ovT)


def _attn2_kernel(bits_ref, slopes_ref, qT_ref, ks_ref, vsT_ref, sel_ref, gates_ref, part_ref,
                  o_ref, m_s, l_s, acc_s, *, tq, nw):
    bi, g, i = pl.program_id(0), pl.program_id(1), pl.program_id(2)
    q0 = i * tq
    t_row = q0 + lax.broadcasted_iota(jnp.int32, (1, tq), 1)
    m_s[...] = jnp.full(m_s.shape, NEG, F32)
    l_s[...] = jnp.zeros(l_s.shape, F32)
    acc_s[...] = jnp.zeros(acc_s.shape, F32)
    base = ((bi * pl.num_programs(1) + g) * pl.num_programs(2) + i) * nw
    n_blk = (q0 + tq) // SEL_BLOCK

    def body(j, carry):
        word = bits_ref[base + jnp.right_shift(j, 5)]
        bit = jnp.bitwise_and(jnp.right_shift(word, jnp.bitwise_and(j, 31)), 1)

        @pl.when(bit == 1)
        def _():
            kk = ks_ref[0, 0, j]
            vT = vsT_ref[0, 0, j]
            dist = t_row - (j * SEL_BLOCK + lax.broadcasted_iota(jnp.int32, (SEL_BLOCK, 1), 0))
            valid = (sel_ref[0, 0, pl.ds(j, 1), :] > 0.0) & (dist >= 0)
            distf = dist.astype(F32)
            for h in range(NSA_GROUP):
                slope = slopes_ref[g * NSA_GROUP + h]
                s = jnp.dot(kk, qT_ref[0, 0, h], preferred_element_type=F32) - slope * distf
                m, l, acc = _online_update(s, valid, m_s[h], l_s[h], acc_s[h], vT)
                m_s[h] = m
                l_s[h] = l
                acc_s[h] = acc

        return carry

    lax.fori_loop(0, n_blk, body, 0)
    gsig = jax.nn.sigmoid(gates_ref[0, 0])
    for h in range(NSA_GROUP):
        o_ref[0, 0, h] = part_ref[0, 0, h] + gsig[1, h:h + 1, :] * (acc_s[h] * (1.0 / l_s[h]))


def _attn2(bits, slopes, qT, ks, vsT, selm, gatesT, part):
    b, kv, grp, dh, s = qT.shape
    tq = min(ATT_TQ, s)
    nqt = s // tq
    ns = s // SEL_BLOCK
    nw = -(-ns // 32)
    grid_spec = pltpu.PrefetchScalarGridSpec(
        num_scalar_prefetch=1,
        grid=(b, kv, nqt),
        in_specs=[pl.BlockSpec(memory_space=pltpu.SMEM),
                  pl.BlockSpec((1, 1, grp, dh, tq), lambda bi, g, i, bits: (bi, g, 0, 0, i)),
                  pl.BlockSpec((1, 1, ns, SEL_BLOCK, dh), lambda bi, g, i, bits: (bi, g, 0, 0, 0)),
                  pl.BlockSpec((1, 1, ns, dh, SEL_BLOCK), lambda bi, g, i, bits: (bi, g, 0, 0, 0)),
                  pl.BlockSpec((1, 1, ns, tq), lambda bi, g, i, bits: (bi, g, 0, i)),
                  pl.BlockSpec((1, 1, 3, grp, tq), lambda bi, g, i, bits: (bi, g, 0, 0, i)),
                  pl.BlockSpec((1, 1, grp, dh, tq), lambda bi, g, i, bits: (bi, g, 0, 0, i))],
        out_specs=pl.BlockSpec((1, 1, grp, dh, tq), lambda bi, g, i, bits: (bi, g, 0, 0, i)),
        scratch_shapes=[pltpu.VMEM((grp, 1, tq), F32), pltpu.VMEM((grp, 1, tq), F32),
                        pltpu.VMEM((grp, dh, tq), F32)],
    )
    return pl.pallas_call(
        functools.partial(_attn2_kernel, tq=tq, nw=nw),
        grid_spec=grid_spec,
        out_shape=jax.ShapeDtypeStruct((b, kv, grp, dh, s), F32),
        compiler_params=_cparams(("parallel", "parallel", "arbitrary")),
        name="attn2",
    )(bits, slopes, qT, ks, vsT, selm, gatesT, part)


def _alibi_slopes(n):
    return np.array([2.0 ** (-8.0 * (h + 1) / n) for h in range(n)], dtype=np.float32)


def _overlap_T(s):
    ncp = s // CMP_STRIDE
    nc = (s - CMP_LEN) // CMP_STRIDE + 1
    ns = s // SEL_BLOCK
    c_start = np.arange(ncp) * CMP_STRIDE
    s_start = np.arange(ns) * SEL_BLOCK
    ov = ((c_start[None, :] <= s_start[:, None] + SEL_BLOCK - 1) &
          (c_start[None, :] + CMP_LEN - 1 >= s_start[:, None]) & (np.arange(ncp)[None, :] < nc))
    return jnp.asarray(ov.astype(np.float32), dtype=BF16)


def _nsa(q, kvs, ng, cmp_pos, w_ck1, w_ck2, w_cv1, w_cv2):
    b, s, _ = q.shape
    kv, grp, dh = NSA_KV_HEADS, NSA_GROUP, HEAD_DIM
    ns = s // SEL_BLOCK
    kc, vc, ks, vs, kw, vw = [kvs[..., i * KV_WIDTH:(i + 1) * KV_WIDTH] for i in range(6)]
    ckf = _compress(kc, cmp_pos, w_ck1, w_ck2)
    cvf = _compress(vc, cmp_pos, w_cv1, w_cv2)
    ncp = ckf.shape[1]
    heads = lambda v: v.reshape(b, -1, kv, dh)
    ck = heads(ckf).transpose(0, 2, 1, 3)
    cvT = heads(cvf).transpose(0, 2, 3, 1)
    kwh = heads(kw).transpose(0, 2, 1, 3)
    vwT = heads(vw).transpose(0, 2, 3, 1)
    ksb = heads(ks).transpose(0, 2, 1, 3).reshape(b, kv, ns, SEL_BLOCK, dh)
    vsT = heads(vs).reshape(b, ns, SEL_BLOCK, kv, dh).transpose(0, 3, 1, 4, 2)
    qT = q.reshape(b, s, kv, grp, dh).transpose(0, 2, 3, 4, 1)
    gatesT = ng.reshape(b, s, kv, grp, 3).transpose(0, 2, 4, 3, 1)
    slopes = jnp.asarray(_alibi_slopes(NSA_HEADS))

    part, selm, cnt = _attn1(slopes, qT, ck, cvT, kwh, vwT, gatesT, _overlap_T(s))
    nqt = cnt.shape[2]
    nw = -(-ns // 32)
    flags = (cnt[:, :, :, 0, :] > 0.0)
    flags = jnp.pad(flags, ((0, 0), (0, 0), (0, 0), (0, nw * 32 - ns))).reshape(b, kv, nqt, nw, 32)
    words = jnp.sum(jnp.where(flags, jnp.left_shift(jnp.uint32(1), jnp.arange(32, dtype=jnp.uint32)),
                              jnp.uint32(0)), axis=-1, dtype=jnp.uint32)
    bits = lax.bitcast_convert_type(words, jnp.int32).reshape(-1)
    yT = _attn2(bits, slopes, qT, ksb, vsT, selm, gatesT, part)
    return yT.transpose(0, 4, 1, 2, 3).reshape(b, s, NSA_WIDTH).astype(BF16)


CONV_TS = 512
CONV_TC = 1024
HALO = 8


def _conv_kernel(x_ref, xp_ref, w_ref, b_ref, o_ref, ext_ref):
    ts = x_ref.shape[1]
    prev = jnp.where(pl.program_id(1) == 0, 0.0, xp_ref[0])
    ext_ref[0:HALO, :] = prev
    ext_ref[HALO:HALO + ts, :] = x_ref[0]
    y = b_ref[...] + w_ref[CONV_WIDTH - 1:CONV_WIDTH, :] * x_ref[0]
    for k in range(CONV_WIDTH - 1):
        off = HALO - (CONV_WIDTH - 1) + k
        y = y + w_ref[k:k + 1, :] * ext_ref[off:off + ts, :]
    o_ref[0] = y * jax.nn.sigmoid(y)


def _conv(x, w, bias):
    b, s, c = x.shape
    ts = min(CONV_TS, s)
    tc = min(CONV_TC, c)
    hb = ts // HALO
    return pl.pallas_call(
        _conv_kernel,
        grid=(b, s // ts, c // tc),
        in_specs=[pl.BlockSpec((1, ts, tc), lambda bi, i, j: (bi, i, j)),
                  pl.BlockSpec((1, HALO, tc), lambda bi, i, j: (bi, jnp.maximum(i * hb - 1, 0), j)),
                  pl.BlockSpec((CONV_WIDTH, tc), lambda bi, i, j: (0, j)),
                  pl.BlockSpec((1, tc), lambda bi, i, j: (0, j))],
        out_specs=pl.BlockSpec((1, ts, tc), lambda bi, i, j: (bi, i, j)),
        out_shape=jax.ShapeDtypeStruct((b, s, c), F32),
        scratch_shapes=[pltpu.VMEM((ts + HALO, tc), F32)],
        compiler_params=_cparams(("parallel", "parallel", "parallel")),
        name="conv",
    )(x, x, w, bias.reshape(1, c))


def _softplus(x):
    return jnp.maximum(x, 0.0) + jnp.log1p(jnp.exp(-jnp.abs(x)))


def _ssd_kernel(x_ref, bm_ref, cm_ref, bT_ref, z_ref, dt_ref, dtT_ref, dtb_ref, dtbT_ref,
                al_ref, alT_ref, dsk_ref, nw_ref, o_ref, st_ref, y_ref):
    q = SSM_CHUNK

    @pl.when(pl.program_id(1) == 0)
    def _():
        st_ref[...] = jnp.zeros(st_ref.shape, F32)

    dt = _softplus(dt_ref[0][:, :SSM_HEADS] + dtb_ref[...])
    dtT = _softplus(dtT_ref[0] + dtbT_ref[...])
    ri = lax.broadcasted_iota(jnp.int32, (q, q), 0)
    ci = lax.broadcasted_iota(jnp.int32, (q, q), 1)
    tri = ri >= ci
    low = jnp.where(tri, 1.0, 0.0).astype(BF16)
    upp = jnp.where(ri <= ci, 1.0, 0.0).astype(BF16)
    cum = _dot01_l(low, dt * (-jnp.exp(al_ref[...])))
    cumT = _dot01_r(dtT * (-jnp.exp(alT_ref[...])), upp)
    ecum = jnp.exp(cum)
    lane_lo = lax.broadcasted_iota(jnp.int32, (1, LANES), 1) < SSM_HEAD_DIM

    for g in range(SSM_GROUPS):
        nsl = slice(g * SSM_STATE, (g + 1) * SSM_STATE)
        cg = cm_ref[0][:, nsl].astype(BF16)
        bg = bm_ref[0][:, nsl].astype(BF16)
        cb = lax.dot_general(cg, bg, (((1,), (1,)), ((), ())), preferred_element_type=F32)
        bTg = bT_ref[0][nsl, :]
        for pr in range(SSM_HPG // 2):
            hp = g * (SSM_HPG // 2) + pr
            xp = x_ref[0][:, hp * LANES:(hp + 1) * LANES].astype(BF16)
            ys, sts, ecs, els = [], [], [], []
            for k in range(2):
                h = 2 * hp + k
                crow = cumT[h:h + 1, :]
                seg = cum[:, h:h + 1] - crow
                w = cb * jnp.exp(jnp.where(tri, seg, NEG)) * dtT[h:h + 1, :]
                ys.append(jnp.dot(w.astype(BF16), xp, preferred_element_type=F32))
                clast = crow[:, q - 1:q]
                to_end = jnp.exp(clast - crow) * dtT[h:h + 1, :]
                sts.append(jnp.dot((bTg * to_end).astype(BF16), xp, preferred_element_type=F32))
                ecs.append(ecum[:, h:h + 1])
                els.append(jnp.exp(clast))
            st = st_ref[hp]
            y_in = jnp.dot(cg, st.astype(BF16), preferred_element_type=F32)
            y_ref[:, hp * LANES:(hp + 1) * LANES] = (
                jnp.where(lane_lo, ys[0], ys[1]) + y_in * jnp.where(lane_lo, ecs[0], ecs[1]))
            st_ref[hp] = st * jnp.where(lane_lo, els[0], els[1]) + jnp.where(lane_lo, sts[0], sts[1])

    gw = SSM_INNER // SSM_GROUPS
    for g in range(SSM_GROUPS):
        sl = slice(g * gw, (g + 1) * gw)
        z = z_ref[0][:, sl]
        y = (y_ref[:, sl] + dsk_ref[:, sl] * x_ref[0][:, sl]) * (z * jax.nn.sigmoid(z))
        y = y * lax.rsqrt(jnp.mean(y * y, axis=-1, keepdims=True) + EPS)
        o_ref[0, :, sl] = (y * nw_ref[:, sl]).astype(o_ref.dtype)


def _ssd(xbc, z, dts, dt_bias, a_log, d_skip, ssm_norm):
    b, s, _ = xbc.shape
    q = SSM_CHUNK
    gn = SSM_GROUPS * SSM_STATE
    hh = SSM_HEADS
    bT = xbc[..., SSM_INNER:SSM_INNER + gn].transpose(0, 2, 1)
    dtT = dts[..., :hh].transpose(0, 2, 1)
    row = lambda v: v.reshape(1, -1).astype(F32)
    col = lambda v: v.reshape(-1, 1).astype(F32)
    const = lambda shape: pl.BlockSpec(shape, lambda bi, c: (0, 0))
    nb = SSM_INNER // q
    return pl.pallas_call(
        _ssd_kernel,
        grid=(b, s // q),
        in_specs=[pl.BlockSpec((1, q, SSM_INNER), lambda bi, c: (bi, c, 0)),
                  pl.BlockSpec((1, q, gn), lambda bi, c: (bi, c, SSM_INNER // gn)),
                  pl.BlockSpec((1, q, gn), lambda bi, c: (bi, c, SSM_INNER // gn + 1)),
                  pl.BlockSpec((1, gn, q), lambda bi, c: (bi, 0, c)),
                  pl.BlockSpec((1, q, SSM_INNER), lambda bi, c: (bi, c, 0)),
                  pl.BlockSpec((1, q, LANES), lambda bi, c: (bi, c, 0)),
                  pl.BlockSpec((1, hh, q), lambda bi, c: (bi, 0, c)),
                  const((1, hh)), const((hh, 1)), const((1, hh)), const((hh, 1)),
                  const((1, SSM_INNER)), const((1, SSM_INNER))],
        out_specs=pl.BlockSpec((1, q, SSM_INNER), lambda bi, c: (bi, c, 0)),
        out_shape=jax.ShapeDtypeStruct((b, s, SSM_INNER), BF16),
        scratch_shapes=[pltpu.VMEM((hh // 2, SSM_STATE, LANES), F32), pltpu.VMEM((q, SSM_INNER), F32)],
        compiler_params=_cparams(("parallel", "arbitrary")),
        name="ssd",
    )(xbc, xbc, xbc, bT, z, dts, dtT, row(dt_bias), col(dt_bias), row(a_log), col(a_log),
      row(jnp.repeat(d_skip, SSM_HEAD_DIM)), row(ssm_norm))


MERGE_TM = 512


def _merge_kernel(h_ref, ya_ref, ym_ref, mg_ref, wa_ref, ws_ref, wo_ref, post_ref, o_ref):
    d = h_ref.shape[1]
    a = jnp.dot(ya_ref[...], wa_ref[...], preferred_element_type=F32)
    m = jnp.dot(ym_ref[...], ws_ref[...], preferred_element_type=F32)
    gts = jax.nn.sigmoid(mg_ref[...])
    merged = gts[:, :d] * a + gts[:, d:] * m
    out = jnp.dot(merged.astype(BF16), wo_ref[...], preferred_element_type=F32)
    o_ref[...] = h_ref[...] + _rms(out, post_ref[...])


def _merge(h2d, ya, ym, mg, wa, ws, wo, post):
    t, d = h2d.shape
    tm = min(MERGE_TM, t)
    const = lambda shape: pl.BlockSpec(shape, lambda i: (0, 0), pipeline_mode=pl.Buffered(1))
    rows = lambda n: pl.BlockSpec((tm, n), lambda i: (i, 0))
    return pl.pallas_call(
        _merge_kernel,
        grid=(t // tm,),
        in_specs=[rows(d), rows(NSA_WIDTH), rows(SSM_INNER), rows(2 * d),
                  const((NSA_WIDTH, d)), const((SSM_INNER, d)), const((d, d)), const((1, d))],
        out_specs=rows(d),
        out_shape=jax.ShapeDtypeStruct((t, d), F32),
        compiler_params=_cparams(("parallel",)),
        name="merge",
    )(h2d, ya, ym, mg, wa.astype(BF16), ws.astype(BF16), wo.astype(BF16), post.reshape(1, d))


def _mixer(h2d, b, s, mix_pre, w_in, cmp_pos, w_ck1, w_ck2, w_cv1, w_cv2, conv_w, conv_b, dt_bias,
           a_log, d_skip, ssm_norm, w_attn_branch, w_ssm_branch, w_out, mix_post):
    o = 0
    cols = {}
    for name, width in (("q", NSA_WIDTH), ("kv", 6 * KV_WIDTH), ("ng", NSA_HEADS * 3), ("z", SSM_INNER),
                        ("xbc", CONV_DIM), ("dt", SSM_HEADS), ("mg", 2 * D_MODEL)):
        cols[name] = w_in[:, o:o + width]
        o += width
    pad = LANES - SSM_HEADS - NSA_HEADS * 3
    w_small = jnp.concatenate([cols["dt"], cols["ng"], jnp.zeros((D_MODEL, pad), F32)], axis=1)

    q = _norm_proj(h2d, mix_pre, cols["q"], BF16, scale=HEAD_DIM ** -0.5).reshape(b, s, NSA_WIDTH)
    kvs = _norm_proj(h2d, mix_pre, cols["kv"], BF16).reshape(b, s, 6 * KV_WIDTH)
    z = _norm_proj(h2d, mix_pre, cols["z"], F32).reshape(b, s, SSM_INNER)
    xbc = _norm_proj(h2d, mix_pre, cols["xbc"], F32).reshape(b, s, CONV_DIM)
    mg = _norm_proj(h2d, mix_pre, cols["mg"], F32)
    small = _norm_proj(h2d, mix_pre, w_small, F32).reshape(b, s, LANES)
    ng = small[..., SSM_HEADS:SSM_HEADS + NSA_HEADS * 3]

    y_a = _nsa(q, kvs, ng, cmp_pos, w_ck1, w_ck2, w_cv1, w_cv2)
    xbc = _conv(xbc, conv_w, conv_b)
    y_m = _ssd(xbc, z, small, dt_bias, a_log, d_skip, ssm_norm)
    return _merge(h2d, y_a.reshape(b * s, NSA_WIDTH), y_m.reshape(b * s, SSM_INNER), mg,
                  w_attn_branch, w_ssm_branch, w_out, mix_post)


def kernel(x, ffn1_pre, ffn1_gate, ffn1_up, ffn1_down, ffn1_post, mix_pre, w_in, cmp_pos, w_ck1, w_ck2,
           w_cv1, w_cv2, conv_w, conv_b, dt_bias, a_log, d_skip, ssm_norm, w_attn_branch, w_ssm_branch,
           w_out, mix_post, ffn2_pre, ffn2_gate, ffn2_up, ffn2_down, ffn2_post):
    b, s, d = x.shape
    h = x.reshape(b * s, d)
    for l in range(ffn1_pre.shape[0]):
        h = _ffn(h, ffn1_pre[l], ffn1_gate[l], ffn1_up[l], ffn1_down[l], ffn1_post[l])
        h = _mixer(h, b, s, mix_pre[l], w_in[l], cmp_pos[l], w_ck1[l], w_ck2[l], w_cv1[l], w_cv2[l],
                   conv_w[l], conv_b[l], dt_bias[l], a_log[l], d_skip[l], ssm_norm[l],
                   w_attn_branch[l], w_ssm_branch[l], w_out[l], mix_post[l])
        h = _ffn(h, ffn2_pre[l], ffn2_gate[l], ffn2_up[l], ffn2_down[l], ffn2_post[l])
    return h.reshape(b, s, d)
```

```python
import functools

import numpy as np
import jax
import jax.numpy as jnp
from jax import lax
from jax.experimental import pallas as pl
from jax.experimental.pallas import tpu as pltpu

F32 = jnp.float32
BF16 = jnp.bfloat16

D_MODEL = 1024
NSA_HEADS = 16
NSA_KV_HEADS = 4
NSA_GROUP = NSA_HEADS // NSA_KV_HEADS
HEAD_DIM = 64
CMP_LEN = 32
CMP_STRIDE = 16
CMP_HIDDEN = 2 * HEAD_DIM
SEL_BLOCK = 64
SEL_TOPK = 16
WINDOW = 512
NSA_WIDTH = NSA_HEADS * HEAD_DIM
KV_WIDTH = NSA_KV_HEADS * HEAD_DIM
BIG = 1e4
NEG = -1e30

SSM_INNER = 2 * D_MODEL
SSM_HEAD_DIM = 64
SSM_HEADS = SSM_INNER // SSM_HEAD_DIM
SSM_GROUPS = 4
SSM_HPG = SSM_HEADS // SSM_GROUPS
SSM_STATE = 128
CONV_WIDTH = 4
SSM_CHUNK = 256
CONV_DIM = SSM_INNER + 2 * SSM_GROUPS * SSM_STATE
FFN_HIDDEN = 2816
EPS = 1e-6

LANES = 128
VMEM_LIMIT = 56 * 1024 * 1024


def _cparams(sem):
    return pltpu.CompilerParams(dimension_semantics=sem, vmem_limit_bytes=VMEM_LIMIT)


def _rms(x, g):
    return x * lax.rsqrt(jnp.mean(x * x, axis=-1, keepdims=True) + EPS) * g


def _split3(x):
    hi = x.astype(BF16)
    r1 = x - hi.astype(F32)
    mid = r1.astype(BF16)
    lo = (r1 - mid.astype(F32)).astype(BF16)
    return hi, mid, lo


def _dot01_l(a01, x):
    hi, mid, lo = _split3(x)
    d = lambda v: jnp.dot(a01, v, preferred_element_type=F32)
    return d(hi) + (d(mid) + d(lo))


def _dot01_r(x, a01):
    hi, mid, lo = _split3(x)
    d = lambda v: jnp.dot(v, a01, preferred_element_type=F32)
    return d(hi) + (d(mid) + d(lo))


FFN_TM = 256
FFN_HC = 256


def _ffn_kernel(x_ref, pre_ref, wg_ref, wu_ref, wd_ref, post_ref, o_ref):
    x = x_ref[...]
    ub = _rms(x, pre_ref[...]).astype(BF16)
    acc = jnp.zeros(x.shape, F32)
    for c in range(FFN_HIDDEN // FFN_HC):
        sl = slice(c * FFN_HC, (c + 1) * FFN_HC)
        g = jnp.dot(ub, wg_ref[:, sl], preferred_element_type=F32)
        u = jnp.dot(ub, wu_ref[:, sl], preferred_element_type=F32)
        a = (g * jax.nn.sigmoid(g)) * u
        acc = acc + jnp.dot(a.astype(BF16), wd_ref[sl, :], preferred_element_type=F32)
    o_ref[...] = x + 0.5 * _rms(acc, post_ref[...])


def _ffn(h2d, pre, wg, wu, wd, post):
    t, d = h2d.shape
    tm = min(FFN_TM, t)
    const = lambda shape: pl.BlockSpec(shape, lambda i: (0, 0), pipeline_mode=pl.Buffered(1))
    return pl.pallas_call(
        _ffn_kernel,
        grid=(t // tm,),
        in_specs=[pl.BlockSpec((tm, d), lambda i: (i, 0)),
                  const((1, d)), const((d, FFN_HIDDEN)), const((d, FFN_HIDDEN)),
                  const((FFN_HIDDEN, d)), const((1, d))],
        out_specs=pl.BlockSpec((tm, d), lambda i: (i, 0)),
        out_shape=jax.ShapeDtypeStruct((t, d), F32),
        compiler_params=_cparams(("parallel",)),
        name="ffn",
    )(h2d, pre.reshape(1, d), wg.astype(BF16), wu.astype(BF16), wd.astype(BF16), post.reshape(1, d))


PROJ_TM = 1024
PROJ_TN = 512


def _norm_proj_kernel(x_ref, g_ref, w_ref, o_ref, u_ref, *, scale):
    @pl.when(pl.program_id(1) == 0)
    def _():
        u_ref[...] = _rms(x_ref[...], g_ref[...]).astype(BF16)

    acc = jnp.dot(u_ref[...], w_ref[...], preferred_element_type=F32)
    if scale != 1.0:
        acc = acc * scale
    o_ref[...] = acc.astype(o_ref.dtype)


def _norm_proj(h2d, g, w, out_dtype, scale=1.0):
    t, d = h2d.shape
    n = w.shape[1]
    tm = min(PROJ_TM, t)
    tn = min(PROJ_TN, n)
    return pl.pallas_call(
        functools.partial(_norm_proj_kernel, scale=scale),
        grid=(t // tm, n // tn),
        in_specs=[pl.BlockSpec((tm, d), lambda i, j: (i, 0)),
                  pl.BlockSpec((1, d), lambda i, j: (0, 0)),
                  pl.BlockSpec((d, tn), lambda i, j: (0, j))],
        out_specs=pl.BlockSpec((tm, tn), lambda i, j: (i, j)),
        out_shape=jax.ShapeDtypeStruct((t, n), out_dtype),
        scratch_shapes=[pltpu.VMEM((tm, d), BF16)],
        compiler_params=_cparams(("parallel", "arbitrary")),
        name="norm_proj",
    )(h2d, g.reshape(1, d), w.astype(BF16))


def _compress_kernel(r_ref, pt_ref, pb_ref, w1t_ref, w1b_ref, w2_ref, o_ref):
    r = r_ref[0].astype(F32)
    top = jnp.dot((r + pt_ref[...]).astype(BF16), w1t_ref[...], preferred_element_type=F32)
    bot = jnp.dot((r + pb_ref[...]).astype(BF16), w1b_ref[...], preferred_element_type=F32)
    ncp = r.shape[0]
    h = top + pltpu.roll(bot, ncp - 1, 0)
    a = (h * jax.nn.sigmoid(h)).astype(BF16)
    o_ref[0] = jnp.dot(a, w2_ref[...], preferred_element_type=F32).astype(o_ref.dtype)


def _compress(k, pos, w1, w2):
    b, s, _ = k.shape
    ncp = s // CMP_STRIDE
    half = CMP_LEN // 2
    rw = half * KV_WIDTH
    eye = jnp.eye(NSA_KV_HEADS, dtype=F32)
    w1r = w1.reshape(CMP_LEN, HEAD_DIM, CMP_HIDDEN)

    def big(wpart):
        return jnp.einsum('ldj,gk->lgdkj', wpart, eye).reshape(rw, NSA_KV_HEADS * CMP_HIDDEN).astype(BF16)

    def posrow(p):
        return jnp.broadcast_to(p[:, None, :], (half, NSA_KV_HEADS, HEAD_DIM)).reshape(1, rw)

    w2big = jnp.einsum('jd,gk->gjkd', w2, eye).reshape(NSA_KV_HEADS * CMP_HIDDEN, KV_WIDTH).astype(BF16)
    nh = NSA_KV_HEADS * CMP_HIDDEN
    const = lambda shape: pl.BlockSpec(shape, lambda i: (0, 0))
    return pl.pallas_call(
        _compress_kernel,
        grid=(b,),
        in_specs=[pl.BlockSpec((1, ncp, rw), lambda i: (i, 0, 0)),
                  const((1, rw)), const((1, rw)), const((rw, nh)), const((rw, nh)),
                  const((nh, KV_WIDTH))],
        out_specs=pl.BlockSpec((1, ncp, KV_WIDTH), lambda i: (i, 0, 0)),
        out_shape=jax.ShapeDtypeStruct((b, ncp, KV_WIDTH), BF16),
        compiler_params=_cparams(("parallel",)),
        name="compress",
    )(k.reshape(b, ncp, rw), posrow(pos[:half]), posrow(pos[half:]), big(w1r[:half]), big(w1r[half:]), w2big)


ATT_TQ = 128
SUPER = 2 * SEL_BLOCK
SEL_UNROLL = 4
LIST_LANES = LANES


def _softmax_step(state, blocks):
    m, l, acc = state
    cands = []
    for s, rows, _ in blocks:
        hr = s.shape[0] // len(rows)
        for a, r in enumerate(rows):
            cands.append(jnp.max(s[a * hr:(a + 1) * hr], axis=0, keepdims=True) + r)
    m_new = functools.reduce(jnp.maximum, cands, m)
    alpha = jnp.exp(m - m_new)
    l = l * alpha
    acc = acc * alpha
    for s, rows, vT in blocks:
        hr = s.shape[0] // len(rows)
        ps = [jnp.exp(s[a * hr:(a + 1) * hr] - (m_new - r)) for a, r in enumerate(rows)]
        p = ps[0] if len(ps) == 1 else jnp.concatenate(ps, axis=0)
        l = l + jnp.sum(p, axis=0, keepdims=True)
        acc = acc + jnp.dot(vT, p.astype(BF16), preferred_element_type=F32)
    return m_new, l, acc


def _attn1_kernel(qT_ref, ck_ref, cvT_ref, kw_ref, vwT_ref, gates_ref, tc_ref, tw_ref, ovT_ref,
                  part_ref, sel_ref, lst_ref, *, tq, ncp, ns, n_sel):
    i = pl.program_id(2)
    q0 = i * tq
    n = NSA_GROUP * tq
    qT = qT_ref[0, 0, 0]

    start = pl.multiple_of(ncp - i * (tq // CMP_STRIDE), 8)
    s = jnp.dot(ck_ref[0, 0], qT, preferred_element_type=F32) + tc_ref[0, pl.ds(start, ncp), :]
    e = jnp.exp(s - jnp.max(s, axis=0, keepdims=True))
    t_rel = jnp.bitwise_and(lax.broadcasted_iota(jnp.int32, (1, n), 1), tq - 1)
    inv = jnp.where(q0 + t_rel >= CMP_LEN - 1, 1.0 / jnp.sum(e, axis=0, keepdims=True), 0.0)
    p = e * inv
    o_c = jnp.dot(cvT_ref[0, 0], p.astype(BF16), preferred_element_type=F32)
    psum = p[:, 0:tq]
    for h in range(1, NSA_GROUP):
        psum = psum + p[:, h * tq:(h + 1) * tq]

    t_row = q0 + lax.broadcasted_iota(jnp.int32, (1, tq), 1)
    imp = _dot01_l(ovT_ref[...], psum)
    j_col = lax.broadcasted_iota(jnp.int32, (ns, 1), 0)
    j_colf = j_col.astype(F32)
    cur = jnp.right_shift(t_row, SEL_BLOCK.bit_length() - 1)
    forced = (j_col == 0) | (j_col == cur) | (j_col == cur - 1)
    imp = jnp.where(forced, BIG, imp)
    work = jnp.where(j_col <= cur, imp, -BIG)
    selm = jnp.zeros((ns, tq), F32)
    for _ in range(n_sel):
        mx = jnp.max(work, axis=0, keepdims=True)
        idx = jnp.min(jnp.where(work == mx, j_colf, float(ns)), axis=0, keepdims=True)
        pick = j_colf == idx
        selm = jnp.where(pick, 1.0, selm)
        work = jnp.where(pick, -jnp.inf, work)
    selm = jnp.where(j_col <= cur, selm, 0.0)
    sel_ref[0, 0] = selm

    nsb = ns // 2
    any_b = jnp.broadcast_to(jnp.max(selm, axis=1, keepdims=True), (ns, LIST_LANES)).astype(BF16)
    pair = jnp.right_shift(lax.broadcasted_iota(jnp.int32, (nsb, ns), 1), 1) == \
        lax.broadcasted_iota(jnp.int32, (nsb, ns), 0)
    sb_col = lax.broadcasted_iota(jnp.int32, (nsb, 1), 0)
    flag = jnp.dot(jnp.where(pair, 1.0, 0.0).astype(BF16), any_b, preferred_element_type=F32)
    flag = jnp.where((flag > 0.0) & (sb_col < i), 1.0, 0.0)
    low = lax.broadcasted_iota(jnp.int32, (nsb, nsb), 0) >= lax.broadcasted_iota(jnp.int32, (nsb, nsb), 1)
    rank = jnp.dot(jnp.where(low, 1.0, 0.0).astype(BF16), flag.astype(BF16), preferred_element_type=F32)
    k_row = lax.broadcasted_iota(jnp.int32, (1, LIST_LANES), 1)
    hit = (rank == (k_row + 1).astype(F32)) & (flag > 0.0)
    lst = jnp.sum(jnp.where(hit, sb_col.astype(F32), 0.0), axis=0, keepdims=True)
    lst = jnp.where(k_row == LIST_LANES - 1, rank[nsb - 1:nsb, :], lst)
    lst_ref[0, 0, 0] = jnp.broadcast_to(lst, (8, LIST_LANES)).astype(jnp.int32)

    n_kt = WINDOW // tq + 1
    state = (jnp.full((1, n), NEG, F32), jnp.zeros((1, n), F32), jnp.zeros((HEAD_DIM, n), F32))
    for kt in [n_kt - 1] + list(range(n_kt - 1)):
        k0 = q0 - WINDOW + kt * tq
        pen = jnp.where(k0 < 0, NEG, 0.0)
        k0 = pl.multiple_of(jnp.maximum(k0, 0), tq)
        sw = jnp.dot(kw_ref[0, 0, pl.ds(k0, tq), :], qT, preferred_element_type=F32) + \
            tw_ref[0, kt * tq:(kt + 1) * tq, :]
        state = _softmax_step(state, [(sw, [pen], vwT_ref[0, 0, :, pl.ds(k0, tq)])])
    _, l_w, acc_w = state

    gsig = jax.nn.sigmoid(gates_ref[0, 0, 0])
    part_ref[0, 0, 0] = gsig[0:1, :] * o_c + gsig[2:3, :] * (acc_w * (1.0 / l_w))


def _attn1(qT, ck, cvT, kw, vwT, gatesT, t_c, t_w, ovT):
    b, kv, nqt, dh, n = qT.shape
    tq = n // NSA_GROUP
    s = nqt * tq
    ncp = ck.shape[2]
    ns = s // SEL_BLOCK
    n_sel = min(SEL_TOPK, ns)
    kern = functools.partial(_attn1_kernel, tq=tq, ncp=ncp, ns=ns, n_sel=n_sel)
    tile = lambda r: pl.BlockSpec((1, 1, 1, r, n), lambda bi, g, i: (bi, g, i, 0, 0))
    per_bg = lambda r, c: pl.BlockSpec((1, 1, r, c), lambda bi, g, i: (bi, g, 0, 0))
    per_g = lambda r: pl.BlockSpec((1, r, n), lambda bi, g, i: (g, 0, 0))
    return pl.pallas_call(
        kern,
        grid=(b, kv, nqt),
        in_specs=[tile(dh), per_bg(ncp, dh), per_bg(dh, ncp), per_bg(s, dh), per_bg(dh, s), tile(3),
                  per_g(2 * ncp), per_g(WINDOW + tq),
                  pl.BlockSpec((ns, ncp), lambda bi, g, i: (0, 0))],
        out_specs=[tile(dh),
                   pl.BlockSpec((1, 1, ns, tq), lambda bi, g, i: (bi, g, 0, i)),
                   pl.BlockSpec((1, 1, 1, 8, LIST_LANES), lambda bi, g, i: (bi, g, i, 0, 0))],
        out_shape=[jax.ShapeDtypeStruct((b, kv, nqt, dh, n), F32),
                   jax.ShapeDtypeStruct((b, kv, ns, s), F32),
                   jax.ShapeDtypeStruct((b, kv, nqt, 8, LIST_LANES), jnp.int32)],
        compiler_params=_cparams(("parallel", "parallel", "arbitrary")),
        name="attn1",
    )(qT, ck, cvT, kw, vwT, gatesT, t_c, t_w, ovT)


def _attn2_kernel(lst_ref, qT_ref, ks_ref, vsT_ref, sel_ref, gates_ref, ts_ref, srow_ref, part_ref,
                  o_ref, *, tq):
    i = pl.program_id(2)
    q0 = i * tq
    n = NSA_GROUP * tq
    qT = qT_ref[0, 0, 0]
    srow = srow_ref[0]

    def sel_rows(sb, base):
        out = []
        for half in range(2):
            row = jnp.where(sel_ref[0, 0, pl.ds(2 * sb + half, 1), :] > 0.0, 0.0, NEG)
            out.append(jnp.concatenate([row] * NSA_GROUP, axis=1) + base)
        return out

    def scores(sb, table):
        k0 = pl.multiple_of(sb * SUPER, SUPER)
        s = jnp.dot(ks_ref[0, 0, pl.ds(k0, SUPER), :], qT, preferred_element_type=F32) + table
        return s, vsT_ref[0, 0, :, pl.ds(k0, SUPER)]

    s, vT = scores(i, ts_ref[0, 1])
    state = (jnp.full((1, n), NEG, F32), jnp.zeros((1, n), F32), jnp.zeros((HEAD_DIM, n), F32))
    state = _softmax_step(state, [(s, sel_rows(i, 0.0), vT)])

    cnt = lst_ref[0, 0, 0, 0, LIST_LANES - 1]

    def body(it, st):
        blocks = []
        for u in range(SEL_UNROLL):
            k = it * SEL_UNROLL + u
            sb = lst_ref[0, 0, 0, 0, k]
            pen = jnp.where(k < cnt, 0.0, NEG)
            base = srow * (sb * SUPER - q0).astype(F32) + pen
            s_u, vT_u = scores(sb, ts_ref[0, 0])
            blocks.append((s_u, sel_rows(sb, base), vT_u))
        return _softmax_step(st, blocks)

    n_it = (cnt + SEL_UNROLL - 1) // SEL_UNROLL
    _, l_s, acc_s = lax.fori_loop(0, n_it, body, state)
    gsig = jax.nn.sigmoid(gates_ref[0, 0, 0])
    o_ref[0, 0, 0] = part_ref[0, 0, 0] + gsig[1:2, :] * (acc_s * (1.0 / l_s))


def _attn2(lst, qT, ks, vsT, selm, gatesT, t_s, srow, part):
    b, kv, nqt, dh, n = qT.shape
    tq = n // NSA_GROUP
    s = nqt * tq
    ns = s // SEL_BLOCK
    tile = lambda r: pl.BlockSpec((1, 1, 1, r, n), lambda bi, g, i: (bi, g, i, 0, 0))
    per_bg = lambda r, c: pl.BlockSpec((1, 1, r, c), lambda bi, g, i: (bi, g, 0, 0))
    return pl.pallas_call(
        functools.partial(_attn2_kernel, tq=tq),
        grid=(b, kv, nqt),
        in_specs=[pl.BlockSpec((1, 1, 1, 8, LIST_LANES), lambda bi, g, i: (bi, g, i, 0, 0),
                               memory_space=pltpu.SMEM),
                  tile(dh), per_bg(s, dh), per_bg(dh, s),
                  pl.BlockSpec((1, 1, ns, tq), lambda bi, g, i: (bi, g, 0, i)),
                  tile(3),
                  pl.BlockSpec((1, 2, SUPER, n), lambda bi, g, i: (g, 0, 0, 0)),
                  pl.BlockSpec((1, 1, n), lambda bi, g, i: (g, 0, 0)),
                  tile(dh)],
        out_specs=tile(dh),
        out_shape=jax.ShapeDtypeStruct((b, kv, nqt, dh, n), F32),
        compiler_params=_cparams(("parallel", "parallel", "arbitrary")),
        name="attn2",
    )(lst, qT, ks, vsT, selm, gatesT, t_s, srow, part)


def _alibi_slopes(n):
    return np.array([2.0 ** (-8.0 * (h + 1) / n) for h in range(n)], dtype=np.float32)


def _overlap_T(s):
    ncp = s // CMP_STRIDE
    nc = (s - CMP_LEN) // CMP_STRIDE + 1
    ns = s // SEL_BLOCK
    c_start = np.arange(ncp) * CMP_STRIDE
    s_start = np.arange(ns) * SEL_BLOCK
    ov = ((c_start[None, :] <= s_start[:, None] + SEL_BLOCK - 1) &
          (c_start[None, :] + CMP_LEN - 1 >= s_start[:, None]) & (np.arange(ncp)[None, :] < nc))
    return jnp.asarray(ov.astype(np.float32), dtype=BF16)


def _bias_tables(s):
    tq = ATT_TQ
    ncp = s // CMP_STRIDE
    slopes = jnp.asarray(_alibi_slopes(NSA_HEADS)).reshape(NSA_KV_HEADS, NSA_GROUP)
    srow = jnp.repeat(slopes, tq, axis=1)[:, None, :]
    t_rel = jnp.tile(jnp.arange(tq, dtype=jnp.int32), NSA_GROUP)[None, None, :]

    def table(rel_pos, lo, hi):
        dist = t_rel - rel_pos[None, :, None]
        bias = -srow * dist.astype(F32)
        return jnp.where((dist >= lo) & (dist < hi), bias, NEG)

    far = 1 << 30
    crel = (jnp.arange(2 * ncp, dtype=jnp.int32) - ncp) * CMP_STRIDE + (CMP_LEN - 1)
    t_c = table(crel, 0, far)
    t_w = table(jnp.arange(WINDOW + tq, dtype=jnp.int32) - WINDOW, 0, WINDOW)
    blk = jnp.arange(SUPER, dtype=jnp.int32)
    t_s = jnp.stack([table(blk, -far, far), table(blk, 0, far)], axis=1)
    return t_c, t_w, t_s, srow


def _nsa(q, kvs, ng, cmp_pos, w_ck1, w_ck2, w_cv1, w_cv2):
    b, s, _ = q.shape
    kv, grp, dh = NSA_KV_HEADS, NSA_GROUP, HEAD_DIM
    tq = ATT_TQ
    nqt = s // tq
    kc, vc, ks, vs, kw, vw = [kvs[..., i * KV_WIDTH:(i + 1) * KV_WIDTH] for i in range(6)]
    ckf = _compress(kc, cmp_pos, w_ck1, w_ck2)
    cvf = _compress(vc, cmp_pos, w_cv1, w_cv2)
    heads = lambda v: v.reshape(b, -1, kv, dh)
    rows = lambda v: heads(v).transpose(0, 2, 1, 3)
    cols = lambda v: heads(v).transpose(0, 2, 3, 1)
    tiles = lambda v, c: v.reshape(b, nqt, tq, kv, grp, c).transpose(0, 3, 1, 5, 4, 2).reshape(
        b, kv, nqt, c, grp * tq)
    qT = tiles(q, dh)
    gatesT = tiles(ng, 3)
    t_c, t_w, t_s, srow = _bias_tables(s)

    part, selm, lst = _attn1(qT, rows(ckf), cols(cvf), rows(kw), cols(vw), gatesT, t_c, t_w, _overlap_T(s))
    y = _attn2(lst, qT, rows(ks), cols(vs), selm, gatesT, t_s, srow, part)
    y = y.reshape(b, kv, nqt, dh, grp, tq).transpose(0, 2, 5, 1, 4, 3)
    return y.reshape(b, s, NSA_WIDTH).astype(BF16)


CONV_TS = 512
CONV_TC = 1024
HALO = 8


def _conv_kernel(x_ref, xp_ref, w_ref, b_ref, o_ref, ext_ref):
    ts = x_ref.shape[1]
    prev = jnp.where(pl.program_id(1) == 0, 0.0, xp_ref[0])
    ext_ref[0:HALO, :] = prev
    ext_ref[HALO:HALO + ts, :] = x_ref[0]
    y = b_ref[...] + w_ref[CONV_WIDTH - 1:CONV_WIDTH, :] * x_ref[0]
    for k in range(CONV_WIDTH - 1):
        off = HALO - (CONV_WIDTH - 1) + k
        y = y + w_ref[k:k + 1, :] * ext_ref[off:off + ts, :]
    o_ref[0] = y * jax.nn.sigmoid(y)


def _conv(x, w, bias):
    b, s, c = x.shape
    ts = min(CONV_TS, s)
    tc = min(CONV_TC, c)
    hb = ts // HALO
    return pl.pallas_call(
        _conv_kernel,
        grid=(b, s // ts, c // tc),
        in_specs=[pl.BlockSpec((1, ts, tc), lambda bi, i, j: (bi, i, j)),
                  pl.BlockSpec((1, HALO, tc), lambda bi, i, j: (bi, jnp.maximum(i * hb - 1, 0), j)),
                  pl.BlockSpec((CONV_WIDTH, tc), lambda bi, i, j: (0, j)),
                  pl.BlockSpec((1, tc), lambda bi, i, j: (0, j))],
        out_specs=pl.BlockSpec((1, ts, tc), lambda bi, i, j: (bi, i, j)),
        out_shape=jax.ShapeDtypeStruct((b, s, c), F32),
        scratch_shapes=[pltpu.VMEM((ts + HALO, tc), F32)],
        compiler_params=_cparams(("parallel", "parallel", "parallel")),
        name="conv",
    )(x, x, w, bias.reshape(1, c))


def _softplus(x):
    return jnp.maximum(x, 0.0) + jnp.log1p(jnp.exp(-jnp.abs(x)))


def _ssd_kernel(x_ref, bm_ref, cm_ref, bT_ref, z_ref, dt_ref, dtT_ref, dtb_ref, dtbT_ref,
                al_ref, alT_ref, dsk_ref, nw_ref, o_ref, st_ref, y_ref):
    q = SSM_CHUNK

    @pl.when(pl.program_id(1) == 0)
    def _():
        st_ref[...] = jnp.zeros(st_ref.shape, F32)

    dt = _softplus(dt_ref[0][:, :SSM_HEADS] + dtb_ref[...])
    dtT = _softplus(dtT_ref[0] + dtbT_ref[...])
    ri = lax.broadcasted_iota(jnp.int32, (q, q), 0)
    ci = lax.broadcasted_iota(jnp.int32, (q, q), 1)
    tri = ri >= ci
    low = jnp.where(tri, 1.0, 0.0).astype(BF16)
    upp = jnp.where(ri <= ci, 1.0, 0.0).astype(BF16)
    cum = _dot01_l(low, dt * (-jnp.exp(al_ref[...])))
    cumT = _dot01_r(dtT * (-jnp.exp(alT_ref[...])), upp)
    ecum = jnp.exp(cum)
    lane_lo = lax.broadcasted_iota(jnp.int32, (1, LANES), 1) < SSM_HEAD_DIM

    for g in range(SSM_GROUPS):
        nsl = slice(g * SSM_STATE, (g + 1) * SSM_STATE)
        cg = cm_ref[0][:, nsl].astype(BF16)
        bg = bm_ref[0][:, nsl].astype(BF16)
        cb = lax.dot_general(cg, bg, (((1,), (1,)), ((), ())), preferred_element_type=F32)
        bTg = bT_ref[0][nsl, :]
        for pr in range(SSM_HPG // 2):
            hp = g * (SSM_HPG // 2) + pr
            xp = x_ref[0][:, hp * LANES:(hp + 1) * LANES].astype(BF16)
            ys, sts, ecs, els = [], [], [], []
            for k in range(2):
                h = 2 * hp + k
                crow = cumT[h:h + 1, :]
                seg = cum[:, h:h + 1] - crow
                w = cb * jnp.exp(jnp.where(tri, seg, NEG)) * dtT[h:h + 1, :]
                ys.append(jnp.dot(w.astype(BF16), xp, preferred_element_type=F32))
                clast = crow[:, q - 1:q]
                to_end = jnp.exp(clast - crow) * dtT[h:h + 1, :]
                sts.append(jnp.dot((bTg * to_end).astype(BF16), xp, preferred_element_type=F32))
                ecs.append(ecum[:, h:h + 1])
                els.append(jnp.exp(clast))
            st = st_ref[hp]
            y_in = jnp.dot(cg, st.astype(BF16), preferred_element_type=F32)
            y_ref[:, hp * LANES:(hp + 1) * LANES] = (
                jnp.where(lane_lo, ys[0], ys[1]) + y_in * jnp.where(lane_lo, ecs[0], ecs[1]))
            st_ref[hp] = st * jnp.where(lane_lo, els[0], els[1]) + jnp.where(lane_lo, sts[0], sts[1])

    gw = SSM_INNER // SSM_GROUPS
    for g in range(SSM_GROUPS):
        sl = slice(g * gw, (g + 1) * gw)
        z = z_ref[0][:, sl]
        y = (y_ref[:, sl] + dsk_ref[:, sl] * x_ref[0][:, sl]) * (z * jax.nn.sigmoid(z))
        y = y * lax.rsqrt(jnp.mean(y * y, axis=-1, keepdims=True) + EPS)
        o_ref[0, :, sl] = (y * nw_ref[:, sl]).astype(o_ref.dtype)


def _ssd(xbc, z, dts, dt_bias, a_log, d_skip, ssm_norm):
    b, s, _ = xbc.shape
    q = SSM_CHUNK
    gn = SSM_GROUPS * SSM_STATE
    hh = SSM_HEADS
    bT = xbc[..., SSM_INNER:SSM_INNER + gn].transpose(0, 2, 1)
    dtT = dts[..., :hh].transpose(0, 2, 1)
    row = lambda v: v.reshape(1, -1).astype(F32)
    col = lambda v: v.reshape(-1, 1).astype(F32)
    const = lambda shape: pl.BlockSpec(shape, lambda bi, c: (0, 0))
    nb = SSM_INNER // q
    return pl.pallas_call(
        _ssd_kernel,
        grid=(b, s // q),
        in_specs=[pl.BlockSpec((1, q, SSM_INNER), lambda bi, c: (bi, c, 0)),
                  pl.BlockSpec((1, q, gn), lambda bi, c: (bi, c, SSM_INNER // gn)),
                  pl.BlockSpec((1, q, gn), lambda bi, c: (bi, c, SSM_INNER // gn + 1)),
                  pl.BlockSpec((1, gn, q), lambda bi, c: (bi, 0, c)),
                  pl.BlockSpec((1, q, SSM_INNER), lambda bi, c: (bi, c, 0)),
                  pl.BlockSpec((1, q, LANES), lambda bi, c: (bi, c, 0)),
                  pl.BlockSpec((1, hh, q), lambda bi, c: (bi, 0, c)),
                  const((1, hh)), const((hh, 1)), const((1, hh)), const((hh, 1)),
                  const((1, SSM_INNER)), const((1, SSM_INNER))],
        out_specs=pl.BlockSpec((1, q, SSM_INNER), lambda bi, c: (bi, c, 0)),
        out_shape=jax.ShapeDtypeStruct((b, s, SSM_INNER), BF16),
        scratch_shapes=[pltpu.VMEM((hh // 2, SSM_STATE, LANES), F32), pltpu.VMEM((q, SSM_INNER), F32)],
        compiler_params=_cparams(("parallel", "arbitrary")),
        name="ssd",
    )(xbc, xbc, xbc, bT, z, dts, dtT, row(dt_bias), col(dt_bias), row(a_log), col(a_log),
      row(jnp.repeat(d_skip, SSM_HEAD_DIM)), row(ssm_norm))


MERGE_TM = 512


def _merge_kernel(h_ref, ya_ref, ym_ref, mg_ref, wa_ref, ws_ref, wo_ref, post_ref, o_ref):
    d = h_ref.shape[1]
    a = jnp.dot(ya_ref[...], wa_ref[...], preferred_element_type=F32)
    m = jnp.dot(ym_ref[...], ws_ref[...], preferred_element_type=F32)
    gts = jax.nn.sigmoid(mg_ref[...])
    merged = gts[:, :d] * a + gts[:, d:] * m
    out = jnp.dot(merged.astype(BF16), wo_ref[...], preferred_element_type=F32)
    o_ref[...] = h_ref[...] + _rms(out, post_ref[...])


def _merge(h2d, ya, ym, mg, wa, ws, wo, post):
    t, d = h2d.shape
    tm = min(MERGE_TM, t)
    const = lambda shape: pl.BlockSpec(shape, lambda i: (0, 0), pipeline_mode=pl.Buffered(1))
    rows = lambda n: pl.BlockSpec((tm, n), lambda i: (i, 0))
    return pl.pallas_call(
        _merge_kernel,
        grid=(t // tm,),
        in_specs=[rows(d), rows(NSA_WIDTH), rows(SSM_INNER), rows(2 * d),
                  const((NSA_WIDTH, d)), const((SSM_INNER, d)), const((d, d)), const((1, d))],
        out_specs=rows(d),
        out_shape=jax.ShapeDtypeStruct((t, d), F32),
        compiler_params=_cparams(("parallel",)),
        name="merge",
    )(h2d, ya, ym, mg, wa.astype(BF16), ws.astype(BF16), wo.astype(BF16), post.reshape(1, d))


def _mixer(h2d, b, s, mix_pre, w_in, cmp_pos, w_ck1, w_ck2, w_cv1, w_cv2, conv_w, conv_b, dt_bias,
           a_log, d_skip, ssm_norm, w_attn_branch, w_ssm_branch, w_out, mix_post):
    o = 0
    cols = {}
    for name, width in (("q", NSA_WIDTH), ("kv", 6 * KV_WIDTH), ("ng", NSA_HEADS * 3), ("z", SSM_INNER),
                        ("xbc", CONV_DIM), ("dt", SSM_HEADS), ("mg", 2 * D_MODEL)):
        cols[name] = w_in[:, o:o + width]
        o += width
    pad = LANES - SSM_HEADS - NSA_HEADS * 3
    w_small = jnp.concatenate([cols["dt"], cols["ng"], jnp.zeros((D_MODEL, pad), F32)], axis=1)

    q = _norm_proj(h2d, mix_pre, cols["q"], BF16, scale=HEAD_DIM ** -0.5).reshape(b, s, NSA_WIDTH)
    kvs = _norm_proj(h2d, mix_pre, cols["kv"], BF16).reshape(b, s, 6 * KV_WIDTH)
    z = _norm_proj(h2d, mix_pre, cols["z"], F32).reshape(b, s, SSM_INNER)
    xbc = _norm_proj(h2d, mix_pre, cols["xbc"], F32).reshape(b, s, CONV_DIM)
    mg = _norm_proj(h2d, mix_pre, cols["mg"], F32)
    small = _norm_proj(h2d, mix_pre, w_small, F32).reshape(b, s, LANES)
    ng = small[..., SSM_HEADS:SSM_HEADS + NSA_HEADS * 3]

    y_a = _nsa(q, kvs, ng, cmp_pos, w_ck1, w_ck2, w_cv1, w_cv2)
    xbc = _conv(xbc, conv_w, conv_b)
    y_m = _ssd(xbc, z, small, dt_bias, a_log, d_skip, ssm_norm)
    return _merge(h2d, y_a.reshape(b * s, NSA_WIDTH), y_m.reshape(b * s, SSM_INNER), mg,
                  w_attn_branch, w_ssm_branch, w_out, mix_post)


def kernel(x, ffn1_pre, ffn1_gate, ffn1_up, ffn1_down, ffn1_post, mix_pre, w_in, cmp_pos, w_ck1, w_ck2,
           w_cv1, w_cv2, conv_w, conv_b, dt_bias, a_log, d_skip, ssm_norm, w_attn_branch, w_ssm_branch,
           w_out, mix_post, ffn2_pre, ffn2_gate, ffn2_up, ffn2_down, ffn2_post):
    b, s, d = x.shape
    h = x.reshape(b * s, d)
    for l in range(ffn1_pre.shape[0]):
        h = _ffn(h, ffn1_pre[l], ffn1_gate[l], ffn1_up[l], ffn1_down[l], ffn1_post[l])
        h = _mixer(h, b, s, mix_pre[l], w_in[l], cmp_pos[l], w_ck1[l], w_ck2[l], w_cv1[l], w_cv2[l],
                   conv_w[l], conv_b[l], dt_bias[l], a_log[l], d_skip[l], ssm_norm[l],
                   w_attn_branch[l], w_ssm_branch[l], w_out[l], mix_post[l])
        h = _ffn(h, ffn2_pre[l], ffn2_gate[l], ffn2_up[l], ffn2_down[l], ffn2_post[l])
    return h.reshape(b, s, d)
```

```python
import functools

import numpy as np
import jax
import jax.numpy as jnp
from jax import lax
from jax.experimental import pallas as pl
from jax.experimental.pallas import tpu as pltpu

F32 = jnp.float32
BF16 = jnp.bfloat16

D_MODEL = 1024
NSA_HEADS = 16
NSA_KV_HEADS = 4
NSA_GROUP = NSA_HEADS // NSA_KV_HEADS
HEAD_DIM = 64
CMP_LEN = 32
CMP_STRIDE = 16
CMP_HIDDEN = 2 * HEAD_DIM
SEL_BLOCK = 64
SEL_TOPK = 16
WINDOW = 512
NSA_WIDTH = NSA_HEADS * HEAD_DIM
KV_WIDTH = NSA_KV_HEADS * HEAD_DIM
BIG = 1e4
NEG = -1e30

SSM_INNER = 2 * D_MODEL
SSM_HEAD_DIM = 64
SSM_HEADS = SSM_INNER // SSM_HEAD_DIM
SSM_GROUPS = 4
SSM_HPG = SSM_HEADS // SSM_GROUPS
SSM_STATE = 128
CONV_WIDTH = 4
SSM_CHUNK = 256
CONV_DIM = SSM_INNER + 2 * SSM_GROUPS * SSM_STATE
FFN_HIDDEN = 2816
EPS = 1e-6

LANES = 128
VMEM_LIMIT = 56 * 1024 * 1024


def _cparams(sem):
    return pltpu.CompilerParams(dimension_semantics=sem, vmem_limit_bytes=VMEM_LIMIT)


def _rms(x, g):
    return x * lax.rsqrt(jnp.mean(x * x, axis=-1, keepdims=True) + EPS) * g


def _split3(x):
    hi = x.astype(BF16)
    r1 = x - hi.astype(F32)
    mid = r1.astype(BF16)
    lo = (r1 - mid.astype(F32)).astype(BF16)
    return hi, mid, lo


def _dot01_l(a01, x):
    hi, mid, lo = _split3(x)
    d = lambda v: jnp.dot(a01, v, preferred_element_type=F32)
    return d(hi) + (d(mid) + d(lo))


def _dot01_r(x, a01):
    hi, mid, lo = _split3(x)
    d = lambda v: jnp.dot(v, a01, preferred_element_type=F32)
    return d(hi) + (d(mid) + d(lo))


FFN_TM = 256
FFN_HC = 256


def _ffn_kernel(x_ref, pre_ref, wg_ref, wu_ref, wd_ref, post_ref, o_ref):
    x = x_ref[...]
    ub = _rms(x, pre_ref[...]).astype(BF16)
    acc = jnp.zeros(x.shape, F32)
    for c in range(FFN_HIDDEN // FFN_HC):
        sl = slice(c * FFN_HC, (c + 1) * FFN_HC)
        g = jnp.dot(ub, wg_ref[:, sl], preferred_element_type=F32)
        u = jnp.dot(ub, wu_ref[:, sl], preferred_element_type=F32)
        a = (g * jax.nn.sigmoid(g)) * u
        acc = acc + jnp.dot(a.astype(BF16), wd_ref[sl, :], preferred_element_type=F32)
    o_ref[...] = x + 0.5 * _rms(acc, post_ref[...])


def _ffn(h2d, pre, wg, wu, wd, post):
    t, d = h2d.shape
    tm = min(FFN_TM, t)
    const = lambda shape: pl.BlockSpec(shape, lambda i: (0, 0), pipeline_mode=pl.Buffered(1))
    return pl.pallas_call(
        _ffn_kernel,
        grid=(t // tm,),
        in_specs=[pl.BlockSpec((tm, d), lambda i: (i, 0)),
                  const((1, d)), const((d, FFN_HIDDEN)), const((d, FFN_HIDDEN)),
                  const((FFN_HIDDEN, d)), const((1, d))],
        out_specs=pl.BlockSpec((tm, d), lambda i: (i, 0)),
        out_shape=jax.ShapeDtypeStruct((t, d), F32),
        compiler_params=_cparams(("parallel",)),
        name="ffn",
    )(h2d, pre.reshape(1, d), wg.astype(BF16), wu.astype(BF16), wd.astype(BF16), post.reshape(1, d))


PROJ_TM = 1024
PROJ_TN = 512


def _norm_proj_kernel(x_ref, g_ref, w_ref, o_ref, u_ref, *, scale):
    @pl.when(pl.program_id(1) == 0)
    def _():
        u_ref[...] = _rms(x_ref[...], g_ref[...]).astype(BF16)

    acc = jnp.dot(u_ref[...], w_ref[...], preferred_element_type=F32)
    if scale != 1.0:
        acc = acc * scale
    o_ref[...] = acc.astype(o_ref.dtype)


def _norm_proj(h2d, g, w, out_dtype, scale=1.0):
    t, d = h2d.shape
    n = w.shape[1]
    tm = min(PROJ_TM, t)
    tn = min(PROJ_TN, n)
    return pl.pallas_call(
        functools.partial(_norm_proj_kernel, scale=scale),
        grid=(t // tm, n // tn),
        in_specs=[pl.BlockSpec((tm, d), lambda i, j: (i, 0)),
                  pl.BlockSpec((1, d), lambda i, j: (0, 0)),
                  pl.BlockSpec((d, tn), lambda i, j: (0, j))],
        out_specs=pl.BlockSpec((tm, tn), lambda i, j: (i, j)),
        out_shape=jax.ShapeDtypeStruct((t, n), out_dtype),
        scratch_shapes=[pltpu.VMEM((tm, d), BF16)],
        compiler_params=_cparams(("parallel", "arbitrary")),
        name="norm_proj",
    )(h2d, g.reshape(1, d), w.astype(BF16))


def _compress_kernel(r_ref, pt_ref, pb_ref, w1t_ref, w1b_ref, w2_ref, o_ref):
    r = r_ref[0].astype(F32)
    top = jnp.dot((r + pt_ref[...]).astype(BF16), w1t_ref[...], preferred_element_type=F32)
    bot = jnp.dot((r + pb_ref[...]).astype(BF16), w1b_ref[...], preferred_element_type=F32)
    ncp = r.shape[0]
    h = top + pltpu.roll(bot, ncp - 1, 0)
    a = (h * jax.nn.sigmoid(h)).astype(BF16)
    o_ref[0] = jnp.dot(a, w2_ref[...], preferred_element_type=F32).astype(o_ref.dtype)


def _compress(k, pos, w1, w2):
    b, s, _ = k.shape
    ncp = s // CMP_STRIDE
    half = CMP_LEN // 2
    rw = half * KV_WIDTH
    eye = jnp.eye(NSA_KV_HEADS, dtype=F32)
    w1r = w1.reshape(CMP_LEN, HEAD_DIM, CMP_HIDDEN)

    def big(wpart):
        return jnp.einsum('ldj,gk->lgdkj', wpart, eye).reshape(rw, NSA_KV_HEADS * CMP_HIDDEN).astype(BF16)

    def posrow(p):
        return jnp.broadcast_to(p[:, None, :], (half, NSA_KV_HEADS, HEAD_DIM)).reshape(1, rw)

    w2big = jnp.einsum('jd,gk->gjkd', w2, eye).reshape(NSA_KV_HEADS * CMP_HIDDEN, KV_WIDTH).astype(BF16)
    nh = NSA_KV_HEADS * CMP_HIDDEN
    const = lambda shape: pl.BlockSpec(shape, lambda i: (0, 0))
    return pl.pallas_call(
        _compress_kernel,
        grid=(b,),
        in_specs=[pl.BlockSpec((1, ncp, rw), lambda i: (i, 0, 0)),
                  const((1, rw)), const((1, rw)), const((rw, nh)), const((rw, nh)),
                  const((nh, KV_WIDTH))],
        out_specs=pl.BlockSpec((1, ncp, KV_WIDTH), lambda i: (i, 0, 0)),
        out_shape=jax.ShapeDtypeStruct((b, ncp, KV_WIDTH), BF16),
        compiler_params=_cparams(("parallel",)),
        name="compress",
    )(k.reshape(b, ncp, rw), posrow(pos[:half]), posrow(pos[half:]), big(w1r[:half]), big(w1r[half:]), w2big)


ATT_TQ = 128
SUPER = 2 * SEL_BLOCK
SEL_UNROLL = 4
LIST_LANES = LANES
TOPK_TILES = 8


def _softmax_step(state, blocks):
    m, l, acc = state
    cands = []
    for s, rows, _ in blocks:
        hr = s.shape[0] // len(rows)
        for a, r in enumerate(rows):
            cands.append(jnp.max(s[a * hr:(a + 1) * hr], axis=0, keepdims=True) + r)
    m_new = functools.reduce(jnp.maximum, cands, m)
    alpha = jnp.exp(m - m_new)
    l = l * alpha
    acc = acc * alpha
    for s, rows, vT in blocks:
        hr = s.shape[0] // len(rows)
        ps = [jnp.exp(s[a * hr:(a + 1) * hr] - (m_new - r)) for a, r in enumerate(rows)]
        p = ps[0] if len(ps) == 1 else jnp.concatenate(ps, axis=0)
        l = l + jnp.sum(p, axis=0, keepdims=True)
        acc = acc + jnp.dot(vT, p.astype(BF16), preferred_element_type=F32)
    return m_new, l, acc


def _attn1_kernel(qT_ref, ck_ref, cvT_ref, kw_ref, vwT_ref, gates_ref, tc_ref, tw_ref, ovT_ref,
                  part_ref, imp_ref, *, tq, ncp):
    i = pl.program_id(2)
    q0 = i * tq
    n = NSA_GROUP * tq
    qT = qT_ref[0, 0, 0]

    n_kt = WINDOW // tq + 1
    blocks = []
    for kt in range(n_kt):
        k0 = q0 - WINDOW + kt * tq
        pen = jnp.where(k0 < 0, NEG, 0.0)
        k0 = pl.multiple_of(jnp.maximum(k0, 0), tq)
        sw = jnp.dot(kw_ref[0, 0, pl.ds(k0, tq), :], qT, preferred_element_type=F32) + \
            tw_ref[0, kt * tq:(kt + 1) * tq, :]
        blocks.append((sw, [pen], vwT_ref[0, 0, :, pl.ds(k0, tq)]))

    start = pl.multiple_of(ncp - i * (tq // CMP_STRIDE), 8)
    s = jnp.dot(ck_ref[0, 0], qT, preferred_element_type=F32) + tc_ref[0, pl.ds(start, ncp), :]
    e = jnp.exp(s - jnp.max(s, axis=0, keepdims=True))
    t_rel = jnp.bitwise_and(lax.broadcasted_iota(jnp.int32, (1, n), 1), tq - 1)
    inv = jnp.where(q0 + t_rel >= CMP_LEN - 1, 1.0 / jnp.sum(e, axis=0, keepdims=True), 0.0)
    p = e * inv
    o_c = jnp.dot(cvT_ref[0, 0], p.astype(BF16), preferred_element_type=F32)
    psum = p[:, 0:tq]
    for h in range(1, NSA_GROUP):
        psum = psum + p[:, h * tq:(h + 1) * tq]
    imp_ref[0, 0, 0] = _dot01_l(ovT_ref[...], psum)

    state = (jnp.full((1, n), NEG, F32), jnp.zeros((1, n), F32), jnp.zeros((HEAD_DIM, n), F32))
    _, l_w, acc_w = _softmax_step(state, blocks)

    gsig = jax.nn.sigmoid(gates_ref[0, 0, 0])
    part_ref[0, 0, 0] = gsig[0:1, :] * o_c + gsig[2:3, :] * (acc_w * (1.0 / l_w))


def _attn1(qT, ck, cvT, kw, vwT, gatesT, t_c, t_w, ovT):
    b, kv, nqt, dh, n = qT.shape
    tq = n // NSA_GROUP
    s = nqt * tq
    ncp = ck.shape[2]
    ns = s // SEL_BLOCK
    kern = functools.partial(_attn1_kernel, tq=tq, ncp=ncp)
    tile = lambda r, c: pl.BlockSpec((1, 1, 1, r, c), lambda bi, g, i: (bi, g, i, 0, 0))
    per_bg = lambda r, c: pl.BlockSpec((1, 1, r, c), lambda bi, g, i: (bi, g, 0, 0))
    per_g = lambda r: pl.BlockSpec((1, r, n), lambda bi, g, i: (g, 0, 0))
    return pl.pallas_call(
        kern,
        grid=(b, kv, nqt),
        in_specs=[tile(dh, n), per_bg(ncp, dh), per_bg(dh, ncp), per_bg(s, dh), per_bg(dh, s), tile(3, n),
                  per_g(2 * ncp), per_g(WINDOW + tq),
                  pl.BlockSpec((ns, ncp), lambda bi, g, i: (0, 0))],
        out_specs=[tile(dh, n), tile(ns, tq)],
        out_shape=[jax.ShapeDtypeStruct((b, kv, nqt, dh, n), F32),
                   jax.ShapeDtypeStruct((b, kv, nqt, ns, tq), F32)],
        compiler_params=_cparams(("parallel", "parallel", "arbitrary")),
        name="attn1",
    )(qT, ck, cvT, kw, vwT, gatesT, t_c, t_w, ovT)


def _topk_kernel(imp_ref, pair_ref, low_ref, pick_ref, sel_ref, lst_ref, *, ns, n_sel, tq):
    c = pl.program_id(2)
    shape = (ns, TOPK_TILES, tq)
    j = lax.broadcasted_iota(jnp.int32, shape, 0)
    jf = j.astype(F32)
    t = (c * TOPK_TILES + lax.broadcasted_iota(jnp.int32, shape, 1)) * tq + \
        lax.broadcasted_iota(jnp.int32, shape, 2)
    cur = jnp.right_shift(t, SEL_BLOCK.bit_length() - 1)
    valid = j <= cur
    forced = (j == 0) | (j == cur) | (j == cur - 1)
    work = jnp.where(forced, -jnp.inf, jnp.where(valid, imp_ref[0, 0, 0], -BIG))
    for _ in range(n_sel - 3):
        v, k = work, jf
        while v.shape[0] > 1:
            h = v.shape[0] // 2
            hi = v[h:] > v[:h]
            v = jnp.where(hi, v[h:], v[:h])
            k = jnp.where(hi, k[h:], k[:h])
        work = jnp.where(jf == k, -jnp.inf, work)
    selm = jnp.where(valid & (work == -jnp.inf), 1.0, 0.0)
    sel_ref[0, 0, 0] = selm

    nsb = ns // 2
    any_q = jnp.broadcast_to(jnp.max(selm, axis=2, keepdims=True), shape[:2] + (LIST_LANES,))
    any_q = any_q.reshape(ns * TOPK_TILES, LIST_LANES).astype(BF16)
    rows = lax.broadcasted_iota(jnp.int32, (nsb * TOPK_TILES, 1), 0)
    sb = jnp.right_shift(rows, 3)
    tile = c * TOPK_TILES + jnp.bitwise_and(rows, TOPK_TILES - 1)
    flag = jnp.dot(pair_ref[...], any_q, preferred_element_type=F32)
    flag = jnp.where((flag > 0.0) & (sb < tile), 1.0, 0.0)
    rank = jnp.dot(low_ref[...], flag.astype(BF16), preferred_element_type=F32)
    k_row = lax.broadcasted_iota(jnp.int32, (1, LIST_LANES), 1)
    hit = jnp.where((rank == (k_row + 1).astype(F32)) & (flag > 0.0), 1.0, 0.0).astype(BF16)
    lst = jnp.dot(pick_ref[...], hit, preferred_element_type=F32)
    total = rank[(nsb - 1) * TOPK_TILES:nsb * TOPK_TILES, :]
    lst_ref[0, 0, 0] = jnp.where(k_row == LIST_LANES - 1, total, lst).astype(jnp.int32)


def _topk(imp_t):
    b, kv, nc, ns, tiles, tq = imp_t.shape
    nsb = ns // 2
    r = np.arange(nsb * tiles)
    q = np.arange(ns * tiles)
    same = (r[:, None] % tiles) == (q[None, :] % tiles)
    pair = same & ((q[None, :] // tiles) // 2 == (r[:, None] // tiles))
    low = ((r[:, None] % tiles) == (r[None, :] % tiles)) & ((r[None, :] // tiles) <= (r[:, None] // tiles))
    pick = (np.arange(tiles)[:, None] == (r[None, :] % tiles)) * (r[None, :] // tiles)
    const = lambda a: jnp.asarray(a.astype(np.float32), dtype=BF16)
    whole = lambda a: pl.BlockSpec(a.shape, lambda bi, g, c: (0, 0))
    blk = pl.BlockSpec((1, 1, 1, ns, tiles, tq), lambda bi, g, c: (bi, g, c, 0, 0, 0))
    return pl.pallas_call(
        functools.partial(_topk_kernel, ns=ns, n_sel=min(SEL_TOPK, ns), tq=tq),
        grid=(b, kv, nc),
        in_specs=[blk, whole(pair), whole(low), whole(pick)],
        out_specs=[blk, pl.BlockSpec((1, 1, 1, tiles, LIST_LANES), lambda bi, g, c: (bi, g, c, 0, 0))],
        out_shape=[jax.ShapeDtypeStruct(imp_t.shape, F32),
                   jax.ShapeDtypeStruct((b, kv, nc, tiles, LIST_LANES), jnp.int32)],
        compiler_params=_cparams(("parallel", "parallel", "arbitrary")),
        name="topk",
    )(imp_t, const(pair), const(low), const(pick))


def _attn2_kernel(lst_ref, qT_ref, ks_ref, vsT_ref, sel_ref, gates_ref, ts_ref, srow_ref, part_ref,
                  o_ref, *, tq):
    i = pl.program_id(2)
    a = jnp.bitwise_and(i, TOPK_TILES - 1)
    q0 = i * tq
    n = NSA_GROUP * tq
    qT = qT_ref[0, 0, 0]
    srow = srow_ref[0]
    cnt = lst_ref[0, 0, 0, a, LIST_LANES - 1]

    def sel_rows(sb, base):
        out = []
        for half in range(2):
            row = jnp.where(sel_ref[0, 0, 0, 2 * sb + half, pl.ds(a, 1), :] > 0.0, 0.0, NEG)
            out.append(jnp.concatenate([row] * NSA_GROUP, axis=1) + base)
        return out

    def scores(sb, table):
        k0 = pl.multiple_of(sb * SUPER, SUPER)
        s = jnp.dot(ks_ref[0, 0, pl.ds(k0, SUPER), :], qT, preferred_element_type=F32) + table
        return s, vsT_ref[0, 0, :, pl.ds(k0, SUPER)]

    def listed(k):
        sb = lst_ref[0, 0, 0, a, k]
        pen = jnp.where(k < cnt, 0.0, NEG)
        base = srow * (sb * SUPER - q0).astype(F32) + pen
        s, vT = scores(sb, ts_ref[0, 0])
        return s, sel_rows(sb, base), vT

    s, vT = scores(i, ts_ref[0, 1])
    first = [(s, sel_rows(i, 0.0), vT)] + [listed(k) for k in range(SEL_UNROLL - 1)]
    state = (jnp.full((1, n), NEG, F32), jnp.zeros((1, n), F32), jnp.zeros((HEAD_DIM, n), F32))
    state = _softmax_step(state, first)

    def body(it, st):
        k0 = SEL_UNROLL - 1 + it * SEL_UNROLL
        return _softmax_step(st, [listed(k0 + u) for u in range(SEL_UNROLL)])

    n_it = (jnp.maximum(cnt - (SEL_UNROLL - 1), 0) + SEL_UNROLL - 1) // SEL_UNROLL
    _, l_s, acc_s = lax.fori_loop(0, n_it, body, state)
    gsig = jax.nn.sigmoid(gates_ref[0, 0, 0])
    o_ref[0, 0, 0] = part_ref[0, 0, 0] + gsig[1:2, :] * (acc_s * (1.0 / l_s))


def _attn2(lst, qT, ks, vsT, sel_t, gatesT, t_s, srow, part):
    b, kv, nqt, dh, n = qT.shape
    tq = n // NSA_GROUP
    s = nqt * tq
    ns = s // SEL_BLOCK
    tile = lambda r: pl.BlockSpec((1, 1, 1, r, n), lambda bi, g, i: (bi, g, i, 0, 0))
    per_bg = lambda r, c: pl.BlockSpec((1, 1, r, c), lambda bi, g, i: (bi, g, 0, 0))
    return pl.pallas_call(
        functools.partial(_attn2_kernel, tq=tq),
        grid=(b, kv, nqt),
        in_specs=[pl.BlockSpec((1, 1, 1, TOPK_TILES, LIST_LANES), lambda bi, g, i: (bi, g, i // TOPK_TILES, 0, 0),
                               memory_space=pltpu.SMEM),
                  tile(dh), per_bg(s, dh), per_bg(dh, s),
                  pl.BlockSpec((1, 1, 1, ns, TOPK_TILES, tq),
                               lambda bi, g, i: (bi, g, i // TOPK_TILES, 0, 0, 0)),
                  tile(3),
                  pl.BlockSpec((1, 2, SUPER, n), lambda bi, g, i: (g, 0, 0, 0)),
                  pl.BlockSpec((1, 1, n), lambda bi, g, i: (g, 0, 0)),
                  tile(dh)],
        out_specs=tile(dh),
        out_shape=jax.ShapeDtypeStruct((b, kv, nqt, dh, n), F32),
        compiler_params=_cparams(("parallel", "parallel", "arbitrary")),
        name="attn2",
    )(lst, qT, ks, vsT, sel_t, gatesT, t_s, srow, part)


def _alibi_slopes(n):
    return np.array([2.0 ** (-8.0 * (h + 1) / n) for h in range(n)], dtype=np.float32)


def _overlap_T(s):
    ncp = s // CMP_STRIDE
    nc = (s - CMP_LEN) // CMP_STRIDE + 1
    ns = s // SEL_BLOCK
    c_start = np.arange(ncp) * CMP_STRIDE
    s_start = np.arange(ns) * SEL_BLOCK
    ov = ((c_start[None, :] <= s_start[:, None] + SEL_BLOCK - 1) &
          (c_start[None, :] + CMP_LEN - 1 >= s_start[:, None]) & (np.arange(ncp)[None, :] < nc))
    return jnp.asarray(ov.astype(np.float32), dtype=BF16)


def _bias_tables(s):
    tq = ATT_TQ
    ncp = s // CMP_STRIDE
    slopes = jnp.asarray(_alibi_slopes(NSA_HEADS)).reshape(NSA_KV_HEADS, NSA_GROUP)
    srow = jnp.repeat(slopes, tq, axis=1)[:, None, :]
    t_rel = jnp.tile(jnp.arange(tq, dtype=jnp.int32), NSA_GROUP)[None, None, :]

    def table(rel_pos, lo, hi):
        dist = t_rel - rel_pos[None, :, None]
        bias = -srow * dist.astype(F32)
        return jnp.where((dist >= lo) & (dist < hi), bias, NEG)

    far = 1 << 30
    crel = (jnp.arange(2 * ncp, dtype=jnp.int32) - ncp) * CMP_STRIDE + (CMP_LEN - 1)
    t_c = table(crel, 0, far)
    t_w = table(jnp.arange(WINDOW + tq, dtype=jnp.int32) - WINDOW, 0, WINDOW)
    blk = jnp.arange(SUPER, dtype=jnp.int32)
    t_s = jnp.stack([table(blk, -far, far), table(blk, 0, far)], axis=1)
    return t_c, t_w, t_s, srow


def _nsa(q, kvs, ng, cmp_pos, w_ck1, w_ck2, w_cv1, w_cv2):
    b, s, _ = q.shape
    kv, grp, dh = NSA_KV_HEADS, NSA_GROUP, HEAD_DIM
    tq = ATT_TQ
    nqt = s // tq
    ns = s // SEL_BLOCK
    kc, vc, ks, vs, kw, vw = [kvs[..., i * KV_WIDTH:(i + 1) * KV_WIDTH] for i in range(6)]
    ckf = _compress(kc, cmp_pos, w_ck1, w_ck2)
    cvf = _compress(vc, cmp_pos, w_cv1, w_cv2)
    heads = lambda v: v.reshape(b, -1, kv, dh)
    rows = lambda v: heads(v).transpose(0, 2, 1, 3)
    cols = lambda v: heads(v).transpose(0, 2, 3, 1)
    tiles = lambda v, c: v.reshape(b, nqt, tq, kv, grp, c).transpose(0, 3, 1, 5, 4, 2).reshape(
        b, kv, nqt, c, grp * tq)
    qT = tiles(q, dh)
    gatesT = tiles(ng, 3)
    t_c, t_w, t_s, srow = _bias_tables(s)

    part, imp = _attn1(qT, rows(ckf), cols(cvf), rows(kw), cols(vw), gatesT, t_c, t_w, _overlap_T(s))
    imp_t = imp.reshape(b, kv, nqt // TOPK_TILES, TOPK_TILES, ns, tq).transpose(0, 1, 2, 4, 3, 5)
    sel_t, lst = _topk(imp_t)
    y = _attn2(lst, qT, rows(ks), cols(vs), sel_t, gatesT, t_s, srow, part)
    y = y.reshape(b, kv, nqt, dh, grp, tq).transpose(0, 2, 5, 1, 4, 3)
    return y.reshape(b, s, NSA_WIDTH).astype(BF16)


HALO = 8
CONV_SLAB = 512


def _softplus(x):
    return jnp.maximum(x, 0.0) + jnp.log1p(jnp.exp(-jnp.abs(x)))


def _ssd_kernel(raw_ref, z_ref, dt_ref, dtT_ref, cw_ref, cb_ref, dtb_ref, dtbT_ref,
                al_ref, alT_ref, dsk_ref, nw_ref, o_ref, st_ref, tail_ref, ext_ref, xc_ref, y_ref):
    q = SSM_CHUNK

    @pl.when(pl.program_id(1) == 0)
    def _():
        st_ref[...] = jnp.zeros(st_ref.shape, F32)
        tail_ref[...] = jnp.zeros(tail_ref.shape, F32)

    ext_ref[0:HALO, :] = tail_ref[...]
    ext_ref[HALO:HALO + q, :] = raw_ref[0].astype(F32)
    tail_ref[...] = ext_ref[q:q + HALO, :]
    for c0 in range(0, CONV_DIM, CONV_SLAB):
        sl = slice(c0, c0 + CONV_SLAB)
        y = cb_ref[:, sl]
        for k in range(CONV_WIDTH):
            off = HALO - (CONV_WIDTH - 1) + k
            y = y + cw_ref[k:k + 1, sl] * ext_ref[off:off + q, sl]
        xc_ref[:, sl] = y * jax.nn.sigmoid(y)

    dt = _softplus(dt_ref[0][:, :SSM_HEADS] + dtb_ref[...])
    dtT = _softplus(dtT_ref[0] + dtbT_ref[...])
    ri = lax.broadcasted_iota(jnp.int32, (q, q), 0)
    ci = lax.broadcasted_iota(jnp.int32, (q, q), 1)
    tri = ri >= ci
    low = jnp.where(tri, 1.0, 0.0).astype(BF16)
    upp = jnp.where(ri <= ci, 1.0, 0.0).astype(BF16)
    cum = _dot01_l(low, dt * (-jnp.exp(al_ref[...])))
    cumT = _dot01_r(dtT * (-jnp.exp(alT_ref[...])), upp)
    ecum = jnp.exp(cum)
    lane_lo = lax.broadcasted_iota(jnp.int32, (1, LANES), 1) < SSM_HEAD_DIM
    b0 = SSM_INNER
    c0 = SSM_INNER + SSM_GROUPS * SSM_STATE

    for g in range(SSM_GROUPS):
        bgf = xc_ref[:, b0 + g * SSM_STATE:b0 + (g + 1) * SSM_STATE]
        cg = xc_ref[:, c0 + g * SSM_STATE:c0 + (g + 1) * SSM_STATE].astype(BF16)
        cb = lax.dot_general(cg, bgf.astype(BF16), (((1,), (1,)), ((), ())), preferred_element_type=F32)
        bTg = bgf.T
        for pr in range(SSM_HPG // 2):
            hp = g * (SSM_HPG // 2) + pr
            xp = xc_ref[:, hp * LANES:(hp + 1) * LANES].astype(BF16)
            ys, sts, ecs, els = [], [], [], []
            for k in range(2):
                h = 2 * hp + k
                crow = cumT[h:h + 1, :]
                seg = cum[:, h:h + 1] - crow
                w = cb * jnp.exp(jnp.where(tri, seg, NEG)) * dtT[h:h + 1, :]
                ys.append(jnp.dot(w.astype(BF16), xp, preferred_element_type=F32))
                clast = crow[:, q - 1:q]
                to_end = jnp.exp(clast - crow) * dtT[h:h + 1, :]
                sts.append(jnp.dot((bTg * to_end).astype(BF16), xp, preferred_element_type=F32))
                ecs.append(ecum[:, h:h + 1])
                els.append(jnp.exp(clast))
            st = st_ref[hp]
            y_in = jnp.dot(cg, st.astype(BF16), preferred_element_type=F32)
            y_ref[:, hp * LANES:(hp + 1) * LANES] = (
                jnp.where(lane_lo, ys[0], ys[1]) + y_in * jnp.where(lane_lo, ecs[0], ecs[1]))
            st_ref[hp] = st * jnp.where(lane_lo, els[0], els[1]) + jnp.where(lane_lo, sts[0], sts[1])

    gw = SSM_INNER // SSM_GROUPS
    for g in range(SSM_GROUPS):
        sl = slice(g * gw, (g + 1) * gw)
        z = z_ref[0][:, sl].astype(F32)
        y = (y_ref[:, sl] + dsk_ref[:, sl] * xc_ref[:, sl]) * (z * jax.nn.sigmoid(z))
        y = y * lax.rsqrt(jnp.mean(y * y, axis=-1, keepdims=True) + EPS)
        o_ref[0, :, sl] = (y * nw_ref[:, sl]).astype(o_ref.dtype)


def _ssd(xbc, z, dts, conv_w, conv_b, dt_bias, a_log, d_skip, ssm_norm):
    b, s, _ = xbc.shape
    q = SSM_CHUNK
    hh = SSM_HEADS
    dtT = dts[..., :hh].transpose(0, 2, 1)
    row = lambda v: v.reshape(1, -1).astype(F32)
    col = lambda v: v.reshape(-1, 1).astype(F32)
    const = lambda shape: pl.BlockSpec(shape, lambda bi, c: (0, 0))
    return pl.pallas_call(
        _ssd_kernel,
        grid=(b, s // q),
        in_specs=[pl.BlockSpec((1, q, CONV_DIM), lambda bi, c: (bi, c, 0)),
                  pl.BlockSpec((1, q, SSM_INNER), lambda bi, c: (bi, c, 0)),
                  pl.BlockSpec((1, q, LANES), lambda bi, c: (bi, c, 0)),
                  pl.BlockSpec((1, hh, q), lambda bi, c: (bi, 0, c)),
                  const((CONV_WIDTH, CONV_DIM)), const((1, CONV_DIM)),
                  const((1, hh)), const((hh, 1)), const((1, hh)), const((hh, 1)),
                  const((1, SSM_INNER)), const((1, SSM_INNER))],
        out_specs=pl.BlockSpec((1, q, SSM_INNER), lambda bi, c: (bi, c, 0)),
        out_shape=jax.ShapeDtypeStruct((b, s, SSM_INNER), BF16),
        scratch_shapes=[pltpu.VMEM((hh // 2, SSM_STATE, LANES), F32),
                        pltpu.VMEM((HALO, CONV_DIM), F32),
                        pltpu.VMEM((q + HALO, CONV_DIM), F32),
                        pltpu.VMEM((q, CONV_DIM), F32),
                        pltpu.VMEM((q, SSM_INNER), F32)],
        compiler_params=_cparams(("parallel", "arbitrary")),
        name="ssd",
    )(xbc, z, dts, dtT, conv_w, row(conv_b), row(dt_bias), col(dt_bias), row(a_log), col(a_log),
      row(jnp.repeat(d_skip, SSM_HEAD_DIM)), row(ssm_norm))


MERGE_TM = 512


def _merge_kernel(h_ref, ya_ref, ym_ref, mg_ref, wa_ref, ws_ref, wo_ref, post_ref, o_ref):
    d = h_ref.shape[1]
    a = jnp.dot(ya_ref[...], wa_ref[...], preferred_element_type=F32)
    m = jnp.dot(ym_ref[...], ws_ref[...], preferred_element_type=F32)
    gts = jax.nn.sigmoid(mg_ref[...].astype(F32))
    merged = gts[:, :d] * a + gts[:, d:] * m
    out = jnp.dot(merged.astype(BF16), wo_ref[...], preferred_element_type=F32)
    o_ref[...] = h_ref[...] + _rms(out, post_ref[...])


def _merge(h2d, ya, ym, mg, wa, ws, wo, post):
    t, d = h2d.shape
    tm = min(MERGE_TM, t)
    const = lambda shape: pl.BlockSpec(shape, lambda i: (0, 0), pipeline_mode=pl.Buffered(1))
    rows = lambda n: pl.BlockSpec((tm, n), lambda i: (i, 0))
    return pl.pallas_call(
        _merge_kernel,
        grid=(t // tm,),
        in_specs=[rows(d), rows(NSA_WIDTH), rows(SSM_INNER), rows(2 * d),
                  const((NSA_WIDTH, d)), const((SSM_INNER, d)), const((d, d)), const((1, d))],
        out_specs=rows(d),
        out_shape=jax.ShapeDtypeStruct((t, d), F32),
        compiler_params=_cparams(("parallel",)),
        name="merge",
    )(h2d, ya, ym, mg, wa.astype(BF16), ws.astype(BF16), wo.astype(BF16), post.reshape(1, d))


def _mixer(h2d, b, s, mix_pre, w_in, cmp_pos, w_ck1, w_ck2, w_cv1, w_cv2, conv_w, conv_b, dt_bias,
           a_log, d_skip, ssm_norm, w_attn_branch, w_ssm_branch, w_out, mix_post):
    o = 0
    cols = {}
    for name, width in (("q", NSA_WIDTH), ("kv", 6 * KV_WIDTH), ("ng", NSA_HEADS * 3), ("z", SSM_INNER),
                        ("xbc", CONV_DIM), ("dt", SSM_HEADS), ("mg", 2 * D_MODEL)):
        cols[name] = w_in[:, o:o + width]
        o += width
    pad = LANES - SSM_HEADS - NSA_HEADS * 3
    w_small = jnp.concatenate([cols["dt"], cols["ng"], jnp.zeros((D_MODEL, pad), F32)], axis=1)

    q = _norm_proj(h2d, mix_pre, cols["q"], BF16, scale=HEAD_DIM ** -0.5).reshape(b, s, NSA_WIDTH)
    kvs = _norm_proj(h2d, mix_pre, cols["kv"], BF16).reshape(b, s, 6 * KV_WIDTH)
    z = _norm_proj(h2d, mix_pre, cols["z"], BF16).reshape(b, s, SSM_INNER)
    xbc = _norm_proj(h2d, mix_pre, cols["xbc"], BF16).reshape(b, s, CONV_DIM)
    mg = _norm_proj(h2d, mix_pre, cols["mg"], BF16)
    small = _norm_proj(h2d, mix_pre, w_small, F32).reshape(b, s, LANES)
    ng = small[..., SSM_HEADS:SSM_HEADS + NSA_HEADS * 3]

    y_a = _nsa(q, kvs, ng, cmp_pos, w_ck1, w_ck2, w_cv1, w_cv2)
    y_m = _ssd(xbc, z, small, conv_w, conv_b, dt_bias, a_log, d_skip, ssm_norm)
    return _merge(h2d, y_a.reshape(b * s, NSA_WIDTH), y_m.reshape(b * s, SSM_INNER), mg,
                  w_attn_branch, w_ssm_branch, w_out, mix_post)


def kernel(x, ffn1_pre, ffn1_gate, ffn1_up, ffn1_down, ffn1_post, mix_pre, w_in, cmp_pos, w_ck1, w_ck2,
           w_cv1, w_cv2, conv_w, conv_b, dt_bias, a_log, d_skip, ssm_norm, w_attn_branch, w_ssm_branch,
           w_out, mix_post, ffn2_pre, ffn2_gate, ffn2_up, ffn2_down, ffn2_post):
    b, s, d = x.shape
    h = x.reshape(b * s, d)
    for l in range(ffn1_pre.shape[0]):
        h = _ffn(h, ffn1_pre[l], ffn1_gate[l], ffn1_up[l], ffn1_down[l], ffn1_post[l])
        h = _mixer(h, b, s, mix_pre[l], w_in[l], cmp_pos[l], w_ck1[l], w_ck2[l], w_cv1[l], w_cv2[l],
                   conv_w[l], conv_b[l], dt_bias[l], a_log[l], d_skip[l], ssm_norm[l],
                   w_attn_branch[l], w_ssm_branch[l], w_out[l], mix_post[l])
        h = _ffn(h, ffn2_pre[l], ffn2_gate[l], ffn2_up[l], ffn2_down[l], ffn2_post[l])
    return h.reshape(b, s, d)
```

```python
import functools

import numpy as np
import jax
import jax.numpy as jnp
from jax import lax
from jax.experimental import pallas as pl
from jax.experimental.pallas import tpu as pltpu

F32 = jnp.float32
BF16 = jnp.bfloat16

D_MODEL = 1024
NSA_HEADS = 16
NSA_KV_HEADS = 4
NSA_GROUP = NSA_HEADS // NSA_KV_HEADS
HEAD_DIM = 64
CMP_LEN = 32
CMP_STRIDE = 16
CMP_HIDDEN = 2 * HEAD_DIM
SEL_BLOCK = 64
SEL_TOPK = 16
WINDOW = 512
NSA_WIDTH = NSA_HEADS * HEAD_DIM
KV_WIDTH = NSA_KV_HEADS * HEAD_DIM
BIG = 1e4
NEG = -1e30

SSM_INNER = 2 * D_MODEL
SSM_HEAD_DIM = 64
SSM_HEADS = SSM_INNER // SSM_HEAD_DIM
SSM_GROUPS = 4
SSM_HPG = SSM_HEADS // SSM_GROUPS
SSM_STATE = 128
CONV_WIDTH = 4
SSM_CHUNK = 256
CONV_DIM = SSM_INNER + 2 * SSM_GROUPS * SSM_STATE
FFN_HIDDEN = 2816
EPS = 1e-6
LOG2E = 1.4426950408889634

LANES = 128
VMEM_LIMIT = 56 * 1024 * 1024


def _cparams(sem):
    return pltpu.CompilerParams(dimension_semantics=sem, vmem_limit_bytes=VMEM_LIMIT)


def _rms(x, g):
    return x * lax.rsqrt(jnp.mean(x * x, axis=-1, keepdims=True) + EPS) * g


def _split3(x):
    hi = x.astype(BF16)
    r1 = x - hi.astype(F32)
    mid = r1.astype(BF16)
    lo = (r1 - mid.astype(F32)).astype(BF16)
    return hi, mid, lo


def _dot01_l(a01, x):
    hi, mid, lo = _split3(x)
    d = lambda v: jnp.dot(a01, v, preferred_element_type=F32)
    return d(hi) + (d(mid) + d(lo))


def _dot01_r(x, a01):
    hi, mid, lo = _split3(x)
    d = lambda v: jnp.dot(v, a01, preferred_element_type=F32)
    return d(hi) + (d(mid) + d(lo))


FFN_TM = 256
FFN_HC = 256


def _ffn_kernel(x_ref, pre_ref, wg_ref, wu_ref, wd_ref, post_ref, o_ref):
    x = x_ref[...]
    ub = _rms(x, pre_ref[...]).astype(BF16)
    acc = jnp.zeros(x.shape, F32)
    for c in range(FFN_HIDDEN // FFN_HC):
        sl = slice(c * FFN_HC, (c + 1) * FFN_HC)
        g = jnp.dot(ub, wg_ref[:, sl], preferred_element_type=F32)
        u = jnp.dot(ub, wu_ref[:, sl], preferred_element_type=F32)
        a = (g * jax.nn.sigmoid(g)) * u
        acc = acc + jnp.dot(a.astype(BF16), wd_ref[sl, :], preferred_element_type=F32)
    o_ref[...] = x + 0.5 * _rms(acc, post_ref[...])


def _ffn(h2d, pre, wg, wu, wd, post):
    t, d = h2d.shape
    tm = min(FFN_TM, t)
    const = lambda shape: pl.BlockSpec(shape, lambda i: (0, 0), pipeline_mode=pl.Buffered(1))
    return pl.pallas_call(
        _ffn_kernel,
        grid=(t // tm,),
        in_specs=[pl.BlockSpec((tm, d), lambda i: (i, 0)),
                  const((1, d)), const((d, FFN_HIDDEN)), const((d, FFN_HIDDEN)),
                  const((FFN_HIDDEN, d)), const((1, d))],
        out_specs=pl.BlockSpec((tm, d), lambda i: (i, 0)),
        out_shape=jax.ShapeDtypeStruct((t, d), F32),
        compiler_params=_cparams(("parallel",)),
        name="ffn",
    )(h2d, pre.reshape(1, d), wg.astype(BF16), wu.astype(BF16), wd.astype(BF16), post.reshape(1, d))


PROJ_TM = 1024
PROJ_TN = 512


def _rmsnorm_kernel(x_ref, g_ref, o_ref):
    o_ref[...] = _rms(x_ref[...], g_ref[...]).astype(o_ref.dtype)


def _rmsnorm(h2d, g):
    t, d = h2d.shape
    tm = min(PROJ_TM, t)
    return pl.pallas_call(
        _rmsnorm_kernel,
        grid=(t // tm,),
        in_specs=[pl.BlockSpec((tm, d), lambda i: (i, 0)), pl.BlockSpec((1, d), lambda i: (0, 0))],
        out_specs=pl.BlockSpec((tm, d), lambda i: (i, 0)),
        out_shape=jax.ShapeDtypeStruct((t, d), BF16),
        compiler_params=_cparams(("parallel",)),
        name="rmsnorm",
    )(h2d, g.reshape(1, d))


def _proj_kernel(u_ref, w_ref, o_ref, *, scale):
    acc = jnp.dot(u_ref[...], w_ref[...], preferred_element_type=F32)
    if scale != 1.0:
        acc = acc * scale
    o_ref[...] = acc.astype(o_ref.dtype)


def _proj(u, w, out_dtype, scale=1.0):
    t, d = u.shape
    n = w.shape[1]
    tm = min(PROJ_TM, t)
    tn = min(PROJ_TN, n)
    return pl.pallas_call(
        functools.partial(_proj_kernel, scale=scale),
        grid=(t // tm, n // tn),
        in_specs=[pl.BlockSpec((tm, d), lambda i, j: (i, 0)),
                  pl.BlockSpec((d, tn), lambda i, j: (0, j))],
        out_specs=pl.BlockSpec((tm, tn), lambda i, j: (i, j)),
        out_shape=jax.ShapeDtypeStruct((t, n), out_dtype),
        compiler_params=_cparams(("parallel", "parallel")),
        name="proj",
    )(u, w.astype(BF16))


def _compress_kernel(r_ref, pt_ref, pb_ref, w1t_ref, w1b_ref, w2_ref, o_ref):
    r = r_ref[0].astype(F32)
    top = jnp.dot((r + pt_ref[...]).astype(BF16), w1t_ref[...], preferred_element_type=F32)
    bot = jnp.dot((r + pb_ref[...]).astype(BF16), w1b_ref[...], preferred_element_type=F32)
    ncp = r.shape[0]
    h = top + pltpu.roll(bot, ncp - 1, 0)
    a = (h * jax.nn.sigmoid(h)).astype(BF16)
    o_ref[0] = jnp.dot(a, w2_ref[...], preferred_element_type=F32).astype(o_ref.dtype)


def _compress(k, pos, w1, w2):
    b, s, _ = k.shape
    ncp = s // CMP_STRIDE
    half = CMP_LEN // 2
    rw = half * KV_WIDTH
    eye = jnp.eye(NSA_KV_HEADS, dtype=F32)
    w1r = w1.reshape(CMP_LEN, HEAD_DIM, CMP_HIDDEN)

    def big(wpart):
        return jnp.einsum('ldj,gk->lgdkj', wpart, eye).reshape(rw, NSA_KV_HEADS * CMP_HIDDEN).astype(BF16)

    def posrow(p):
        return jnp.broadcast_to(p[:, None, :], (half, NSA_KV_HEADS, HEAD_DIM)).reshape(1, rw)

    w2big = jnp.einsum('jd,gk->gjkd', w2, eye).reshape(NSA_KV_HEADS * CMP_HIDDEN, KV_WIDTH).astype(BF16)
    nh = NSA_KV_HEADS * CMP_HIDDEN
    const = lambda shape: pl.BlockSpec(shape, lambda i: (0, 0))
    return pl.pallas_call(
        _compress_kernel,
        grid=(b,),
        in_specs=[pl.BlockSpec((1, ncp, rw), lambda i: (i, 0, 0)),
                  const((1, rw)), const((1, rw)), const((rw, nh)), const((rw, nh)),
                  const((nh, KV_WIDTH))],
        out_specs=pl.BlockSpec((1, ncp, KV_WIDTH), lambda i: (i, 0, 0)),
        out_shape=jax.ShapeDtypeStruct((b, ncp, KV_WIDTH), BF16),
        compiler_params=_cparams(("parallel",)),
        name="compress",
    )(k.reshape(b, ncp, rw), posrow(pos[:half]), posrow(pos[half:]), big(w1r[:half]), big(w1r[half:]), w2big)


ATT_TQ = 128
SUPER = 2 * SEL_BLOCK
SEL_UNROLL = 5
LIST_LANES = LANES
TOPK_TILES = 8


def _softmax_step(state, blocks):
    m, l8, acc = state
    n = m.shape[1]
    part = None
    for s, rows, _ in blocks:
        hr = s.shape[0] // len(rows)
        for a, r in enumerate(rows):
            c = jnp.max(s[a * hr:(a + 1) * hr].reshape(hr // 8, 8, n), axis=0) + r
            part = c if part is None else jnp.maximum(part, c)
    m_new = jnp.maximum(m, jnp.max(part, axis=0, keepdims=True))
    alpha = jnp.exp2(m - m_new)
    l8 = l8 * alpha
    acc = acc * alpha
    for s, rows, vT in blocks:
        hr = s.shape[0] // len(rows)
        ps = [jnp.exp2(s[a * hr:(a + 1) * hr] - (m_new - r)) for a, r in enumerate(rows)]
        p = ps[0] if len(ps) == 1 else jnp.concatenate(ps, axis=0)
        l8 = l8 + jnp.sum(p.reshape(p.shape[0] // 8, 8, n), axis=0)
        acc = acc + jnp.dot(vT, p.astype(BF16), preferred_element_type=F32)
    return m_new, l8, acc


def _softmax_init(n):
    return (jnp.full((1, n), NEG, F32), jnp.zeros((8, n), F32), jnp.zeros((HEAD_DIM, n), F32))


def _attn1_kernel(qT_ref, ck_ref, cvT_ref, kw_ref, vwT_ref, gates_ref, tc_ref, tw_ref, ovT_ref,
                  part_ref, imp_ref, *, tq, ncp):
    i = pl.program_id(2)
    q0 = i * tq
    n = NSA_GROUP * tq
    qT = qT_ref[0, 0, 0]

    n_kt = WINDOW // tq + 1
    blocks = []
    for kt in range(n_kt):
        k0 = q0 - WINDOW + kt * tq
        pen = jnp.where(k0 < 0, NEG, 0.0)
        k0 = pl.multiple_of(jnp.maximum(k0, 0), tq)
        sw = jnp.dot(kw_ref[0, 0, pl.ds(k0, tq), :], qT, preferred_element_type=F32) + \
            tw_ref[0, kt * tq:(kt + 1) * tq, :]
        blocks.append((sw, [pen], vwT_ref[0, 0, :, pl.ds(k0, tq)]))

    start = pl.multiple_of(ncp - i * (tq // CMP_STRIDE), 8)
    s = jnp.dot(ck_ref[0, 0], qT, preferred_element_type=F32) + tc_ref[0, pl.ds(start, ncp), :]
    e = jnp.exp2(s - jnp.max(s, axis=0, keepdims=True))
    t_rel = jnp.bitwise_and(lax.broadcasted_iota(jnp.int32, (1, n), 1), tq - 1)
    inv = jnp.where(q0 + t_rel >= CMP_LEN - 1, 1.0 / jnp.sum(e, axis=0, keepdims=True), 0.0)
    p = e * inv
    o_c = jnp.dot(cvT_ref[0, 0], p.astype(BF16), preferred_element_type=F32)
    psum = p[:, 0:tq]
    for h in range(1, NSA_GROUP):
        psum = psum + p[:, h * tq:(h + 1) * tq]
    imp_ref[0, 0, 0] = _dot01_l(ovT_ref[...], psum)

    _, l_w, acc_w = _softmax_step(_softmax_init(n), blocks)
    l_w = jnp.sum(l_w, axis=0, keepdims=True)

    gsig = jax.nn.sigmoid(gates_ref[0, 0, 0])
    part_ref[0, 0, 0] = gsig[0:1, :] * o_c + gsig[2:3, :] * (acc_w * (1.0 / l_w))


def _attn1(qT, ck, cvT, kw, vwT, gatesT, t_c, t_w, ovT):
    b, kv, nqt, dh, n = qT.shape
    tq = n // NSA_GROUP
    s = nqt * tq
    ncp = ck.shape[2]
    ns = s // SEL_BLOCK
    kern = functools.partial(_attn1_kernel, tq=tq, ncp=ncp)
    tile = lambda r, c: pl.BlockSpec((1, 1, 1, r, c), lambda bi, g, i: (bi, g, i, 0, 0))
    per_bg = lambda r, c: pl.BlockSpec((1, 1, r, c), lambda bi, g, i: (bi, g, 0, 0))
    per_g = lambda r: pl.BlockSpec((1, r, n), lambda bi, g, i: (g, 0, 0))
    return pl.pallas_call(
        kern,
        grid=(b, kv, nqt),
        in_specs=[tile(dh, n), per_bg(ncp, dh), per_bg(dh, ncp), per_bg(s, dh), per_bg(dh, s), tile(3, n),
                  per_g(2 * ncp), per_g(WINDOW + tq),
                  pl.BlockSpec((ns, ncp), lambda bi, g, i: (0, 0))],
        out_specs=[tile(dh, n), tile(ns, tq)],
        out_shape=[jax.ShapeDtypeStruct((b, kv, nqt, dh, n), F32),
                   jax.ShapeDtypeStruct((b, kv, nqt, ns, tq), F32)],
        compiler_params=_cparams(("parallel", "parallel", "arbitrary")),
        name="attn1",
    )(qT, ck, cvT, kw, vwT, gatesT, t_c, t_w, ovT)


def _topk_kernel(imp_ref, pair_ref, low_ref, pick_ref, sel_ref, lst_ref, *, ns, n_sel, tq):
    c = pl.program_id(2)
    shape = (ns, TOPK_TILES, tq)
    j = lax.broadcasted_iota(jnp.int32, shape, 0)
    jf = j.astype(F32)
    t = (c * TOPK_TILES + lax.broadcasted_iota(jnp.int32, shape, 1)) * tq + \
        lax.broadcasted_iota(jnp.int32, shape, 2)
    cur = jnp.right_shift(t, SEL_BLOCK.bit_length() - 1)
    valid = j <= cur
    forced = (j == 0) | (j == cur) | (j == cur - 1)
    work = jnp.where(forced, -jnp.inf, jnp.where(valid, imp_ref[0, 0, 0], -BIG))
    for _ in range(n_sel - 3):
        v, k = work, jf
        while v.shape[0] > 1:
            h = v.shape[0] // 2
            hi = v[h:] > v[:h]
            v = jnp.where(hi, v[h:], v[:h])
            k = jnp.where(hi, k[h:], k[:h])
        work = jnp.where(jf == k, -jnp.inf, work)
    selm = jnp.where(valid & (work == -jnp.inf), 1.0, 0.0)
    sel_ref[0, 0, 0] = selm

    nsb = ns // 2
    any_q = jnp.broadcast_to(jnp.max(selm, axis=2, keepdims=True), shape[:2] + (LIST_LANES,))
    any_q = any_q.reshape(ns * TOPK_TILES, LIST_LANES).astype(BF16)
    rows = lax.broadcasted_iota(jnp.int32, (nsb * TOPK_TILES, 1), 0)
    sb = jnp.right_shift(rows, 3)
    tile = c * TOPK_TILES + jnp.bitwise_and(rows, TOPK_TILES - 1)
    flag = jnp.dot(pair_ref[...], any_q, preferred_element_type=F32)
    flag = jnp.where((flag > 0.0) & (sb < tile), 1.0, 0.0)
    rank = jnp.dot(low_ref[...], flag.astype(BF16), preferred_element_type=F32)
    k_row = lax.broadcasted_iota(jnp.int32, (1, LIST_LANES), 1)
    hit = jnp.where((rank == (k_row + 1).astype(F32)) & (flag > 0.0), 1.0, 0.0).astype(BF16)
    lst = jnp.dot(pick_ref[...], hit, preferred_element_type=F32)
    total = rank[(nsb - 1) * TOPK_TILES:nsb * TOPK_TILES, :]
    lst_ref[0, 0, 0] = jnp.where(k_row == LIST_LANES - 1, total, lst).astype(jnp.int32)


def _topk(imp_t):
    b, kv, nc, ns, tiles, tq = imp_t.shape
    nsb = ns // 2
    r = np.arange(nsb * tiles)
    q = np.arange(ns * tiles)
    same = (r[:, None] % tiles) == (q[None, :] % tiles)
    pair = same & ((q[None, :] // tiles) // 2 == (r[:, None] // tiles))
    low = ((r[:, None] % tiles) == (r[None, :] % tiles)) & ((r[None, :] // tiles) <= (r[:, None] // tiles))
    pick = (np.arange(tiles)[:, None] == (r[None, :] % tiles)) * (r[None, :] // tiles)
    const = lambda a: jnp.asarray(a.astype(np.float32), dtype=BF16)
    whole = lambda a: pl.BlockSpec(a.shape, lambda bi, g, c: (0, 0))
    blk = pl.BlockSpec((1, 1, 1, ns, tiles, tq), lambda bi, g, c: (bi, g, c, 0, 0, 0))
    return pl.pallas_call(
        functools.partial(_topk_kernel, ns=ns, n_sel=min(SEL_TOPK, ns), tq=tq),
        grid=(b, kv, nc),
        in_specs=[blk, whole(pair), whole(low), whole(pick)],
        out_specs=[blk, pl.BlockSpec((1, 1, 1, tiles, LIST_LANES), lambda bi, g, c: (bi, g, c, 0, 0))],
        out_shape=[jax.ShapeDtypeStruct(imp_t.shape, F32),
                   jax.ShapeDtypeStruct((b, kv, nc, tiles, LIST_LANES), jnp.int32)],
        compiler_params=_cparams(("parallel", "parallel", "arbitrary")),
        name="topk",
    )(imp_t, const(pair), const(low), const(pick))


def _attn2_kernel(lst_ref, qT_ref, ks_ref, vsT_ref, sel_ref, gates_ref, ts_ref, srow_ref, part_ref,
                  o_ref, *, tq):
    i = pl.program_id(2)
    a = jnp.bitwise_and(i, TOPK_TILES - 1)
    q0 = i * tq
    n = NSA_GROUP * tq
    qT = qT_ref[0, 0, 0]
    srow = srow_ref[0]
    cnt = lst_ref[0, 0, 0, a, LIST_LANES - 1]

    def sel_rows(sb, base):
        out = []
        for half in range(2):
            row = jnp.where(sel_ref[0, 0, 0, 2 * sb + half, pl.ds(a, 1), :] > 0.0, 0.0, NEG)
            out.append(jnp.concatenate([row] * NSA_GROUP, axis=1) + base)
        return out

    def scores(sb, table):
        k0 = pl.multiple_of(sb * SUPER, SUPER)
        s = jnp.dot(ks_ref[0, 0, pl.ds(k0, SUPER), :], qT, preferred_element_type=F32) + table
        return s, vsT_ref[0, 0, :, pl.ds(k0, SUPER)]

    def listed(k):
        sb = lst_ref[0, 0, 0, a, k]
        pen = jnp.where(k < cnt, 0.0, NEG)
        base = srow * (sb * SUPER - q0).astype(F32) + pen
        s, vT = scores(sb, ts_ref[0, 0])
        return s, sel_rows(sb, base), vT

    s, vT = scores(i, ts_ref[0, 1])
    first = [(s, sel_rows(i, 0.0), vT)] + [listed(k) for k in range(SEL_UNROLL - 1)]
    state = _softmax_step(_softmax_init(n), first)

    def body(it, st):
        k0 = SEL_UNROLL - 1 + it * SEL_UNROLL
        return _softmax_step(st, [listed(k0 + u) for u in range(SEL_UNROLL)])

    n_it = (jnp.maximum(cnt - (SEL_UNROLL - 1), 0) + SEL_UNROLL - 1) // SEL_UNROLL
    _, l_s, acc_s = lax.fori_loop(0, n_it, body, state)
    l_s = jnp.sum(l_s, axis=0, keepdims=True)
    gsig = jax.nn.sigmoid(gates_ref[0, 0, 0])
    o_ref[0, 0, 0] = (part_ref[0, 0, 0] + gsig[1:2, :] * (acc_s * (1.0 / l_s))).astype(o_ref.dtype)


def _attn2(lst, qT, ks, vsT, sel_t, gatesT, t_s, srow, part):
    b, kv, nqt, dh, n = qT.shape
    tq = n // NSA_GROUP
    s = nqt * tq
    ns = s // SEL_BLOCK
    tile = lambda r: pl.BlockSpec((1, 1, 1, r, n), lambda bi, g, i: (bi, g, i, 0, 0))
    per_bg = lambda r, c: pl.BlockSpec((1, 1, r, c), lambda bi, g, i: (bi, g, 0, 0))
    return pl.pallas_call(
        functools.partial(_attn2_kernel, tq=tq),
        grid=(b, kv, nqt),
        in_specs=[pl.BlockSpec((1, 1, 1, TOPK_TILES, LIST_LANES), lambda bi, g, i: (bi, g, i // TOPK_TILES, 0, 0),
                               memory_space=pltpu.SMEM),
                  tile(dh), per_bg(s, dh), per_bg(dh, s),
                  pl.BlockSpec((1, 1, 1, ns, TOPK_TILES, tq),
                               lambda bi, g, i: (bi, g, i // TOPK_TILES, 0, 0, 0)),
                  tile(3),
                  pl.BlockSpec((1, 2, SUPER, n), lambda bi, g, i: (g, 0, 0, 0)),
                  pl.BlockSpec((1, 1, n), lambda bi, g, i: (g, 0, 0)),
                  tile(dh)],
        out_specs=tile(dh),
        out_shape=jax.ShapeDtypeStruct((b, kv, nqt, dh, n), BF16),
        compiler_params=_cparams(("parallel", "parallel", "arbitrary")),
        name="attn2",
    )(lst, qT, ks, vsT, sel_t, gatesT, t_s, srow, part)


def _alibi_slopes(n):
    return np.array([2.0 ** (-8.0 * (h + 1) / n) for h in range(n)], dtype=np.float32)


def _overlap_T(s):
    ncp = s // CMP_STRIDE
    nc = (s - CMP_LEN) // CMP_STRIDE + 1
    ns = s // SEL_BLOCK
    c_start = np.arange(ncp) * CMP_STRIDE
    s_start = np.arange(ns) * SEL_BLOCK
    ov = ((c_start[None, :] <= s_start[:, None] + SEL_BLOCK - 1) &
          (c_start[None, :] + CMP_LEN - 1 >= s_start[:, None]) & (np.arange(ncp)[None, :] < nc))
    return jnp.asarray(ov.astype(np.float32), dtype=BF16)


def _bias_tables(s):
    tq = ATT_TQ
    ncp = s // CMP_STRIDE
    slopes = jnp.asarray(_alibi_slopes(NSA_HEADS)).reshape(NSA_KV_HEADS, NSA_GROUP)
    srow = jnp.repeat(slopes, tq, axis=1)[:, None, :] * LOG2E
    t_rel = jnp.tile(jnp.arange(tq, dtype=jnp.int32), NSA_GROUP)[None, None, :]

    def table(rel_pos, lo, hi):
        dist = t_rel - rel_pos[None, :, None]
        bias = -srow * dist.astype(F32)
        return jnp.where((dist >= lo) & (dist < hi), bias, NEG)

    far = 1 << 30
    crel = (jnp.arange(2 * ncp, dtype=jnp.int32) - ncp) * CMP_STRIDE + (CMP_LEN - 1)
    t_c = table(crel, 0, far)
    t_w = table(jnp.arange(WINDOW + tq, dtype=jnp.int32) - WINDOW, 0, WINDOW)
    blk = jnp.arange(SUPER, dtype=jnp.int32)
    t_s = jnp.stack([table(blk, -far, far), table(blk, 0, far)], axis=1)
    return t_c, t_w, t_s, srow


def _nsa(q, kvs, ng, cmp_pos, w_ck1, w_ck2, w_cv1, w_cv2):
    b, s, _ = q.shape
    kv, grp, dh = NSA_KV_HEADS, NSA_GROUP, HEAD_DIM
    tq = ATT_TQ
    nqt = s // tq
    ns = s // SEL_BLOCK
    kc, vc, ks, vs, kw, vw = [kvs[..., i * KV_WIDTH:(i + 1) * KV_WIDTH] for i in range(6)]
    ckf = _compress(kc, cmp_pos, w_ck1, w_ck2)
    cvf = _compress(vc, cmp_pos, w_cv1, w_cv2)
    heads = lambda v: v.reshape(b, -1, kv, dh)
    rows = lambda v: heads(v).transpose(0, 2, 1, 3)
    cols = lambda v: heads(v).transpose(0, 2, 3, 1)
    tiles = lambda v, c: v.reshape(b, nqt, tq, kv, grp, c).transpose(0, 3, 1, 5, 4, 2).reshape(
        b, kv, nqt, c, grp * tq)
    qT = tiles(q, dh)
    gatesT = tiles(ng, 3)
    t_c, t_w, t_s, srow = _bias_tables(s)

    part, imp = _attn1(qT, rows(ckf), cols(cvf), rows(kw), cols(vw), gatesT, t_c, t_w, _overlap_T(s))
    imp_t = imp.reshape(b, kv, nqt // TOPK_TILES, TOPK_TILES, ns, tq).transpose(0, 1, 2, 4, 3, 5)
    sel_t, lst = _topk(imp_t)
    y = _attn2(lst, qT, rows(ks), cols(vs), sel_t, gatesT, t_s, srow, part)
    y = y.reshape(b, kv, nqt, dh, grp, tq).transpose(0, 2, 5, 1, 4, 3)
    return y.reshape(b, s, NSA_WIDTH)


HALO = 8
CONV_SLAB = 512


def _softplus(x):
    return jnp.maximum(x, 0.0) + jnp.log1p(jnp.exp(-jnp.abs(x)))


def _ssd_kernel(raw_ref, z_ref, dt_ref, dtT_ref, cw_ref, cb_ref, dtb_ref, dtbT_ref,
                al_ref, alT_ref, dsk_ref, nw_ref, o_ref, st_ref, tail_ref, ext_ref, xc_ref, y_ref):
    q = SSM_CHUNK

    @pl.when(pl.program_id(1) == 0)
    def _():
        st_ref[...] = jnp.zeros(st_ref.shape, F32)
        tail_ref[...] = jnp.zeros(tail_ref.shape, F32)

    ext_ref[0:HALO, :] = tail_ref[...]
    ext_ref[HALO:HALO + q, :] = raw_ref[0].astype(F32)
    tail_ref[...] = ext_ref[q:q + HALO, :]
    for c0 in range(0, CONV_DIM, CONV_SLAB):
        sl = slice(c0, c0 + CONV_SLAB)
        y = cb_ref[:, sl]
        for k in range(CONV_WIDTH):
            off = HALO - (CONV_WIDTH - 1) + k
            y = y + cw_ref[k:k + 1, sl] * ext_ref[off:off + q, sl]
        xc_ref[:, sl] = y * jax.nn.sigmoid(y)

    dt = _softplus(dt_ref[0][:, :SSM_HEADS] + dtb_ref[...])
    dtT = _softplus(dtT_ref[0] + dtbT_ref[...])
    ri = lax.broadcasted_iota(jnp.int32, (q, q), 0)
    ci = lax.broadcasted_iota(jnp.int32, (q, q), 1)
    tri = ri >= ci
    low = jnp.where(tri, 1.0, 0.0).astype(BF16)
    upp = jnp.where(ri <= ci, 1.0, 0.0).astype(BF16)
    cum = _dot01_l(low, dt * (-jnp.exp(al_ref[...])))
    cumT = _dot01_r(dtT * (-jnp.exp(alT_ref[...])), upp)
    ecum = jnp.exp(cum)
    lane_lo = lax.broadcasted_iota(jnp.int32, (1, LANES), 1) < SSM_HEAD_DIM
    b0 = SSM_INNER
    c0 = SSM_INNER + SSM_GROUPS * SSM_STATE

    for g in range(SSM_GROUPS):
        bgf = xc_ref[:, b0 + g * SSM_STATE:b0 + (g + 1) * SSM_STATE]
        cg = xc_ref[:, c0 + g * SSM_STATE:c0 + (g + 1) * SSM_STATE].astype(BF16)
        cb = lax.dot_general(cg, bgf.astype(BF16), (((1,), (1,)), ((), ())), preferred_element_type=F32)
        bTg = bgf.T
        for pr in range(SSM_HPG // 2):
            hp = g * (SSM_HPG // 2) + pr
            xp = xc_ref[:, hp * LANES:(hp + 1) * LANES].astype(BF16)
            ys, sts, ecs, els = [], [], [], []
            for k in range(2):
                h = 2 * hp + k
                crow = cumT[h:h + 1, :]
                seg = cum[:, h:h + 1] - crow
                w = cb * jnp.exp(jnp.where(tri, seg, NEG)) * dtT[h:h + 1, :]
                ys.append(jnp.dot(w.astype(BF16), xp, preferred_element_type=F32))
                clast = crow[:, q - 1:q]
                to_end = jnp.exp(clast - crow) * dtT[h:h + 1, :]
                sts.append(jnp.dot((bTg * to_end).astype(BF16), xp, preferred_element_type=F32))
                ecs.append(ecum[:, h:h + 1])
                els.append(jnp.exp(clast))
            st = st_ref[hp]
            y_in = jnp.dot(cg, st.astype(BF16), preferred_element_type=F32)
            y_ref[:, hp * LANES:(hp + 1) * LANES] = (
                jnp.where(lane_lo, ys[0], ys[1]) + y_in * jnp.where(lane_lo, ecs[0], ecs[1]))
            st_ref[hp] = st * jnp.where(lane_lo, els[0], els[1]) + jnp.where(lane_lo, sts[0], sts[1])

    gw = SSM_INNER // SSM_GROUPS
    for g in range(SSM_GROUPS):
        sl = slice(g * gw, (g + 1) * gw)
        z = z_ref[0][:, sl].astype(F32)
        y = (y_ref[:, sl] + dsk_ref[:, sl] * xc_ref[:, sl]) * (z * jax.nn.sigmoid(z))
        y = y * lax.rsqrt(jnp.mean(y * y, axis=-1, keepdims=True) + EPS)
        o_ref[0, :, sl] = (y * nw_ref[:, sl]).astype(o_ref.dtype)


def _ssd(xbc, z, dts, conv_w, conv_b, dt_bias, a_log, d_skip, ssm_norm):
    b, s, _ = xbc.shape
    q = SSM_CHUNK
    hh = SSM_HEADS
    dtT = dts[..., :hh].transpose(0, 2, 1)
    row = lambda v: v.reshape(1, -1).astype(F32)
    col = lambda v: v.reshape(-1, 1).astype(F32)
    const = lambda shape: pl.BlockSpec(shape, lambda bi, c: (0, 0))
    return pl.pallas_call(
        _ssd_kernel,
        grid=(b, s // q),
        in_specs=[pl.BlockSpec((1, q, CONV_DIM), lambda bi, c: (bi, c, 0)),
                  pl.BlockSpec((1, q, SSM_INNER), lambda bi, c: (bi, c, 0)),
                  pl.BlockSpec((1, q, LANES), lambda bi, c: (bi, c, 0)),
                  pl.BlockSpec((1, hh, q), lambda bi, c: (bi, 0, c)),
                  const((CONV_WIDTH, CONV_DIM)), const((1, CONV_DIM)),
                  const((1, hh)), const((hh, 1)), const((1, hh)), const((hh, 1)),
                  const((1, SSM_INNER)), const((1, SSM_INNER))],
        out_specs=pl.BlockSpec((1, q, SSM_INNER), lambda bi, c: (bi, c, 0)),
        out_shape=jax.ShapeDtypeStruct((b, s, SSM_INNER), BF16),
        scratch_shapes=[pltpu.VMEM((hh // 2, SSM_STATE, LANES), F32),
                        pltpu.VMEM((HALO, CONV_DIM), F32),
                        pltpu.VMEM((q + HALO, CONV_DIM), F32),
                        pltpu.VMEM((q, CONV_DIM), F32),
                        pltpu.VMEM((q, SSM_INNER), F32)],
        compiler_params=_cparams(("parallel", "arbitrary")),
        name="ssd",
    )(xbc, z, dts, dtT, conv_w, row(conv_b), row(dt_bias), col(dt_bias), row(a_log), col(a_log),
      row(jnp.repeat(d_skip, SSM_HEAD_DIM)), row(ssm_norm))


MERGE_TM = 512


def _merge_kernel(h_ref, ya_ref, ym_ref, mg_ref, wa_ref, ws_ref, wo_ref, post_ref, o_ref):
    d = h_ref.shape[1]
    a = jnp.dot(ya_ref[...], wa_ref[...], preferred_element_type=F32)
    m = jnp.dot(ym_ref[...], ws_ref[...], preferred_element_type=F32)
    gts = jax.nn.sigmoid(mg_ref[...].astype(F32))
    merged = gts[:, :d] * a + gts[:, d:] * m
    out = jnp.dot(merged.astype(BF16), wo_ref[...], preferred_element_type=F32)
    o_ref[...] = h_ref[...] + _rms(out, post_ref[...])


def _merge(h2d, ya, ym, mg, wa, ws, wo, post):
    t, d = h2d.shape
    tm = min(MERGE_TM, t)
    const = lambda shape: pl.BlockSpec(shape, lambda i: (0, 0), pipeline_mode=pl.Buffered(1))
    rows = lambda n: pl.BlockSpec((tm, n), lambda i: (i, 0))
    return pl.pallas_call(
        _merge_kernel,
        grid=(t // tm,),
        in_specs=[rows(d), rows(NSA_WIDTH), rows(SSM_INNER), rows(2 * d),
                  const((NSA_WIDTH, d)), const((SSM_INNER, d)), const((d, d)), const((1, d))],
        out_specs=rows(d),
        out_shape=jax.ShapeDtypeStruct((t, d), F32),
        compiler_params=_cparams(("parallel",)),
        name="merge",
    )(h2d, ya, ym, mg, wa.astype(BF16), ws.astype(BF16), wo.astype(BF16), post.reshape(1, d))


def _mixer(h2d, b, s, mix_pre, w_in, cmp_pos, w_ck1, w_ck2, w_cv1, w_cv2, conv_w, conv_b, dt_bias,
           a_log, d_skip, ssm_norm, w_attn_branch, w_ssm_branch, w_out, mix_post):
    o = 0
    cols = {}
    for name, width in (("q", NSA_WIDTH), ("kv", 6 * KV_WIDTH), ("ng", NSA_HEADS * 3), ("z", SSM_INNER),
                        ("xbc", CONV_DIM), ("dt", SSM_HEADS), ("mg", 2 * D_MODEL)):
        cols[name] = w_in[:, o:o + width]
        o += width
    pad = LANES - SSM_HEADS - NSA_HEADS * 3
    w_small = jnp.concatenate([cols["dt"], cols["ng"], jnp.zeros((D_MODEL, pad), F32)], axis=1)

    u = _rmsnorm(h2d, mix_pre)
    q = _proj(u, cols["q"], BF16, scale=HEAD_DIM ** -0.5 * LOG2E).reshape(b, s, NSA_WIDTH)
    kvs = _proj(u, cols["kv"], BF16).reshape(b, s, 6 * KV_WIDTH)
    z = _proj(u, cols["z"], BF16).reshape(b, s, SSM_INNER)
    xbc = _proj(u, cols["xbc"], BF16).reshape(b, s, CONV_DIM)
    mg = _proj(u, cols["mg"], BF16)
    small = _proj(u, w_small, F32).reshape(b, s, LANES)
    ng = small[..., SSM_HEADS:SSM_HEADS + NSA_HEADS * 3]

    y_a = _nsa(q, kvs, ng, cmp_pos, w_ck1, w_ck2, w_cv1, w_cv2)
    y_m = _ssd(xbc, z, small, conv_w, conv_b, dt_bias, a_log, d_skip, ssm_norm)
    return _merge(h2d, y_a.reshape(b * s, NSA_WIDTH), y_m.reshape(b * s, SSM_INNER), mg,
                  w_attn_branch, w_ssm_branch, w_out, mix_post)


def kernel(x, ffn1_pre, ffn1_gate, ffn1_up, ffn1_down, ffn1_post, mix_pre, w_in, cmp_pos, w_ck1, w_ck2,
           w_cv1, w_cv2, conv_w, conv_b, dt_bias, a_log, d_skip, ssm_norm, w_attn_branch, w_ssm_branch,
           w_out, mix_post, ffn2_pre, ffn2_gate, ffn2_up, ffn2_down, ffn2_post):
    b, s, d = x.shape
    h = x.reshape(b * s, d)
    for l in range(ffn1_pre.shape[0]):
        h = _ffn(h, ffn1_pre[l], ffn1_gate[l], ffn1_up[l], ffn1_down[l], ffn1_post[l])
        h = _mixer(h, b, s, mix_pre[l], w_in[l], cmp_pos[l], w_ck1[l], w_ck2[l], w_cv1[l], w_cv2[l],
                   conv_w[l], conv_b[l], dt_bias[l], a_log[l], d_skip[l], ssm_norm[l],
                   w_attn_branch[l], w_ssm_branch[l], w_out[l], mix_post[l])
        h = _ffn(h, ffn2_pre[l], ffn2_gate[l], ffn2_up[l], ffn2_down[l], ffn2_post[l])
    return h.reshape(b, s, d)
```

```python
import functools

import numpy as np
import jax
import jax.numpy as jnp
from jax import lax
from jax.experimental import pallas as pl
from jax.experimental.pallas import tpu as pltpu

F32 = jnp.float32
BF16 = jnp.bfloat16

D_MODEL = 1024
NSA_HEADS = 16
NSA_KV_HEADS = 4
NSA_GROUP = NSA_HEADS // NSA_KV_HEADS
HEAD_DIM = 64
CMP_LEN = 32
CMP_STRIDE = 16
CMP_HIDDEN = 2 * HEAD_DIM
SEL_BLOCK = 64
SEL_TOPK = 16
WINDOW = 512
NSA_WIDTH = NSA_HEADS * HEAD_DIM
KV_WIDTH = NSA_KV_HEADS * HEAD_DIM
BIG = 1e4
NEG = -1e30

SSM_INNER = 2 * D_MODEL
SSM_HEAD_DIM = 64
SSM_HEADS = SSM_INNER // SSM_HEAD_DIM
SSM_GROUPS = 4
SSM_HPG = SSM_HEADS // SSM_GROUPS
SSM_STATE = 128
CONV_WIDTH = 4
SSM_CHUNK = 256
CONV_DIM = SSM_INNER + 2 * SSM_GROUPS * SSM_STATE
FFN_HIDDEN = 2816
EPS = 1e-6
LOG2E = 1.4426950408889634

LANES = 128
VMEM_LIMIT = 56 * 1024 * 1024


def _cparams(sem):
    return pltpu.CompilerParams(dimension_semantics=sem, vmem_limit_bytes=VMEM_LIMIT)


def _rms(x, g):
    return x * lax.rsqrt(jnp.mean(x * x, axis=-1, keepdims=True) + EPS) * g


def _split3(x):
    hi = x.astype(BF16)
    r1 = x - hi.astype(F32)
    mid = r1.astype(BF16)
    lo = (r1 - mid.astype(F32)).astype(BF16)
    return hi, mid, lo


def _dot01_l(a01, x):
    hi, mid, lo = _split3(x)
    d = lambda v: jnp.dot(a01, v, preferred_element_type=F32)
    return d(hi) + (d(mid) + d(lo))


def _dot01_r(x, a01):
    hi, mid, lo = _split3(x)
    d = lambda v: jnp.dot(v, a01, preferred_element_type=F32)
    return d(hi) + (d(mid) + d(lo))


FFN_TM = 256
FFN_HC = 256


def _ffn_kernel(x_ref, pre_ref, wg_ref, wu_ref, wd_ref, post_ref, o_ref):
    x = x_ref[...]
    ub = _rms(x, pre_ref[...]).astype(BF16)
    acc = jnp.zeros(x.shape, F32)
    for c in range(FFN_HIDDEN // FFN_HC):
        sl = slice(c * FFN_HC, (c + 1) * FFN_HC)
        g = jnp.dot(ub, wg_ref[:, sl], preferred_element_type=F32)
        u = jnp.dot(ub, wu_ref[:, sl], preferred_element_type=F32)
        a = (g * jax.nn.sigmoid(g)) * u
        acc = acc + jnp.dot(a.astype(BF16), wd_ref[sl, :], preferred_element_type=F32)
    o_ref[...] = x + 0.5 * _rms(acc, post_ref[...])


def _ffn(h2d, pre, wg, wu, wd, post):
    t, d = h2d.shape
    tm = min(FFN_TM, t)
    const = lambda shape: pl.BlockSpec(shape, lambda i: (0, 0), pipeline_mode=pl.Buffered(1))
    return pl.pallas_call(
        _ffn_kernel,
        grid=(t // tm,),
        in_specs=[pl.BlockSpec((tm, d), lambda i: (i, 0)),
                  const((1, d)), const((d, FFN_HIDDEN)), const((d, FFN_HIDDEN)),
                  const((FFN_HIDDEN, d)), const((1, d))],
        out_specs=pl.BlockSpec((tm, d), lambda i: (i, 0)),
        out_shape=jax.ShapeDtypeStruct((t, d), F32),
        compiler_params=_cparams(("parallel",)),
        name="ffn",
    )(h2d, pre.reshape(1, d), wg.astype(BF16), wu.astype(BF16), wd.astype(BF16), post.reshape(1, d))


PROJ_TM = 1024
PROJ_TN = 1024


def _rmsnorm_kernel(x_ref, g_ref, o_ref):
    o_ref[...] = _rms(x_ref[...], g_ref[...]).astype(o_ref.dtype)


def _rmsnorm(h2d, g):
    t, d = h2d.shape
    tm = min(PROJ_TM, t)
    return pl.pallas_call(
        _rmsnorm_kernel,
        grid=(t // tm,),
        in_specs=[pl.BlockSpec((tm, d), lambda i: (i, 0)), pl.BlockSpec((1, d), lambda i: (0, 0))],
        out_specs=pl.BlockSpec((tm, d), lambda i: (i, 0)),
        out_shape=jax.ShapeDtypeStruct((t, d), BF16),
        compiler_params=_cparams(("parallel",)),
        name="rmsnorm",
    )(h2d, g.reshape(1, d))


def _proj_kernel(u_ref, w_ref, o_ref, *, scale):
    acc = jnp.dot(u_ref[...], w_ref[...], preferred_element_type=F32)
    if scale != 1.0:
        acc = acc * scale
    o_ref[...] = acc.astype(o_ref.dtype)


def _proj(u, w, out_dtype, scale=1.0):
    t, d = u.shape
    n = w.shape[1]
    tm = min(PROJ_TM, t)
    tn = next(c for c in range(min(PROJ_TN, n), 0, -LANES) if n % c == 0)
    return pl.pallas_call(
        functools.partial(_proj_kernel, scale=scale),
        grid=(t // tm, n // tn),
        in_specs=[pl.BlockSpec((tm, d), lambda i, j: (i, 0)),
                  pl.BlockSpec((d, tn), lambda i, j: (0, j))],
        out_specs=pl.BlockSpec((tm, tn), lambda i, j: (i, j)),
        out_shape=jax.ShapeDtypeStruct((t, n), out_dtype),
        compiler_params=_cparams(("parallel", "parallel")),
        name="proj",
    )(u, w.astype(BF16))


def _compress_kernel(r_ref, pt_ref, pb_ref, w1t_ref, w1b_ref, w2_ref, o_ref):
    r = r_ref[0].astype(F32)
    top = jnp.dot((r + pt_ref[...]).astype(BF16), w1t_ref[...], preferred_element_type=F32)
    bot = jnp.dot((r + pb_ref[...]).astype(BF16), w1b_ref[...], preferred_element_type=F32)
    ncp = r.shape[0]
    h = top + pltpu.roll(bot, ncp - 1, 0)
    a = (h * jax.nn.sigmoid(h)).astype(BF16)
    o_ref[0] = jnp.dot(a, w2_ref[...], preferred_element_type=F32).astype(o_ref.dtype)


def _compress(k, pos, w1, w2):
    b, s, _ = k.shape
    ncp = s // CMP_STRIDE
    half = CMP_LEN // 2
    rw = half * KV_WIDTH
    eye = jnp.eye(NSA_KV_HEADS, dtype=F32)
    w1r = w1.reshape(CMP_LEN, HEAD_DIM, CMP_HIDDEN)

    def big(wpart):
        return jnp.einsum('ldj,gk->lgdkj', wpart, eye).reshape(rw, NSA_KV_HEADS * CMP_HIDDEN).astype(BF16)

    def posrow(p):
        return jnp.broadcast_to(p[:, None, :], (half, NSA_KV_HEADS, HEAD_DIM)).reshape(1, rw)

    w2big = jnp.einsum('jd,gk->gjkd', w2, eye).reshape(NSA_KV_HEADS * CMP_HIDDEN, KV_WIDTH).astype(BF16)
    nh = NSA_KV_HEADS * CMP_HIDDEN
    const = lambda shape: pl.BlockSpec(shape, lambda i: (0, 0))
    return pl.pallas_call(
        _compress_kernel,
        grid=(b,),
        in_specs=[pl.BlockSpec((1, ncp, rw), lambda i: (i, 0, 0)),
                  const((1, rw)), const((1, rw)), const((rw, nh)), const((rw, nh)),
                  const((nh, KV_WIDTH))],
        out_specs=pl.BlockSpec((1, ncp, KV_WIDTH), lambda i: (i, 0, 0)),
        out_shape=jax.ShapeDtypeStruct((b, ncp, KV_WIDTH), BF16),
        compiler_params=_cparams(("parallel",)),
        name="compress",
    )(k.reshape(b, ncp, rw), posrow(pos[:half]), posrow(pos[half:]), big(w1r[:half]), big(w1r[half:]), w2big)


ATT_TQ = 128
SUPER = 2 * SEL_BLOCK
SEL_UNROLL = 5
LIST_LANES = LANES
TOPK_TILES = 8
ATT_PAIR = 4


def _softmax_steps(states, blocks_list):
    n = states[0][0].shape[1]
    ms, ls, accs = [], [], []
    for (m, l8, acc), blocks in zip(states, blocks_list):
        part = None
        for s, rows, _ in blocks:
            hr = s.shape[0] // len(rows)
            for a, r in enumerate(rows):
                c = jnp.max(s[a * hr:(a + 1) * hr].reshape(hr // 8, 8, n), axis=0) + r
                part = c if part is None else jnp.maximum(part, c)
        m_new = jnp.maximum(m, jnp.max(part, axis=0, keepdims=True))
        alpha = jnp.exp2(m - m_new)
        ms.append(m_new)
        ls.append(l8 * alpha)
        accs.append(acc * alpha)
    for u in range(max(len(blocks) for blocks in blocks_list)):
        for k, blocks in enumerate(blocks_list):
            if u >= len(blocks):
                continue
            s, rows, vT = blocks[u]
            hr = s.shape[0] // len(rows)
            ps = [jnp.exp2(s[a * hr:(a + 1) * hr] - (ms[k] - r)) for a, r in enumerate(rows)]
            p = ps[0] if len(ps) == 1 else jnp.concatenate(ps, axis=0)
            ls[k] = ls[k] + jnp.sum(p.reshape(p.shape[0] // 8, 8, n), axis=0)
            accs[k] = accs[k] + jnp.dot(vT, p.astype(BF16), preferred_element_type=F32)
    return [(ms[k], ls[k], accs[k]) for k in range(len(states))]


def _softmax_step(state, blocks):
    return _softmax_steps([state], [blocks])[0]


def _softmax_init(n):
    return (jnp.full((1, n), NEG, F32), jnp.zeros((8, n), F32), jnp.zeros((HEAD_DIM, n), F32))


def _attn1_tile(t, i, qT_ref, ck_ref, cvT_ref, kw_ref, vwT_ref, gates_ref, tc_ref, tw_ref, ovT_ref,
                part_ref, imp_ref, tq, ncp):
    q0 = i * tq
    n = NSA_GROUP * tq
    qT = qT_ref[0, 0, t]

    n_kt = WINDOW // tq + 1
    blocks = []
    for kt in range(n_kt):
        k0 = q0 - WINDOW + kt * tq
        pen = jnp.where(k0 < 0, NEG, 0.0)
        k0 = pl.multiple_of(jnp.maximum(k0, 0), tq)
        sw = jnp.dot(kw_ref[0, 0, pl.ds(k0, tq), :], qT, preferred_element_type=F32) + \
            tw_ref[0, kt * tq:(kt + 1) * tq, :]
        blocks.append((sw, [pen], vwT_ref[0, 0, :, pl.ds(k0, tq)]))
    yield

    start = pl.multiple_of(ncp - i * (tq // CMP_STRIDE), 8)
    s = jnp.dot(ck_ref[0, 0], qT, preferred_element_type=F32) + tc_ref[0, pl.ds(start, ncp), :]
    yield
    e = jnp.exp2(s - jnp.max(s, axis=0, keepdims=True))
    yield
    t_rel = jnp.bitwise_and(lax.broadcasted_iota(jnp.int32, (1, n), 1), tq - 1)
    inv = jnp.where(q0 + t_rel >= CMP_LEN - 1, 1.0 / jnp.sum(e, axis=0, keepdims=True), 0.0)
    p = e * inv
    o_c = jnp.dot(cvT_ref[0, 0], p.astype(BF16), preferred_element_type=F32)
    yield
    psum = p[:, 0:tq]
    for h in range(1, NSA_GROUP):
        psum = psum + p[:, h * tq:(h + 1) * tq]
    imp_ref[0, 0, t] = _dot01_l(ovT_ref[...], psum)
    yield

    _, l_w, acc_w = _softmax_step(_softmax_init(n), blocks)
    l_w = jnp.sum(l_w, axis=0, keepdims=True)
    yield

    gsig = jax.nn.sigmoid(gates_ref[0, 0, t])
    part_ref[0, 0, t] = gsig[0:1, :] * o_c + gsig[2:3, :] * (acc_w * (1.0 / l_w))
    yield


def _interleave(gens):
    live = list(gens)
    while live:
        live = [g for g in live if next(g, StopIteration) is not StopIteration]


def _attn1_kernel(*refs, tq, ncp):
    i0 = pl.program_id(2) * ATT_PAIR
    _interleave([_attn1_tile(t, i0 + t, *refs, tq, ncp) for t in range(ATT_PAIR)])


def _attn1(qT, ck, cvT, kw, vwT, gatesT, t_c, t_w, ovT):
    b, kv, nqt, dh, n = qT.shape
    tq = n // NSA_GROUP
    s = nqt * tq
    ncp = ck.shape[2]
    ns = s // SEL_BLOCK
    kern = functools.partial(_attn1_kernel, tq=tq, ncp=ncp)
    tile = lambda r, c: pl.BlockSpec((1, 1, ATT_PAIR, r, c), lambda bi, g, i: (bi, g, i, 0, 0))
    per_bg = lambda r, c: pl.BlockSpec((1, 1, r, c), lambda bi, g, i: (bi, g, 0, 0))
    per_g = lambda r: pl.BlockSpec((1, r, n), lambda bi, g, i: (g, 0, 0))
    return pl.pallas_call(
        kern,
        grid=(b, kv, nqt // ATT_PAIR),
        in_specs=[tile(dh, n), per_bg(ncp, dh), per_bg(dh, ncp), per_bg(s, dh), per_bg(dh, s), tile(3, n),
                  per_g(2 * ncp), per_g(WINDOW + tq),
                  pl.BlockSpec((ns, ncp), lambda bi, g, i: (0, 0))],
        out_specs=[tile(dh, n), tile(ns, tq)],
        out_shape=[jax.ShapeDtypeStruct((b, kv, nqt, dh, n), F32),
                   jax.ShapeDtypeStruct((b, kv, nqt, ns, tq), F32)],
        compiler_params=_cparams(("parallel", "parallel", "arbitrary")),
        name="attn1",
    )(qT, ck, cvT, kw, vwT, gatesT, t_c, t_w, ovT)


def _topk_kernel(imp_ref, pair_ref, low_ref, pick_ref, sel_ref, lst_ref, *, ns, n_sel, tq):
    c = pl.program_id(2)
    shape = (ns, TOPK_TILES, tq)
    j = lax.broadcasted_iota(jnp.int32, shape, 0)
    jf = j.astype(F32)
    t = (c * TOPK_TILES + lax.broadcasted_iota(jnp.int32, shape, 1)) * tq + \
        lax.broadcasted_iota(jnp.int32, shape, 2)
    cur = jnp.right_shift(t, SEL_BLOCK.bit_length() - 1)
    valid = j <= cur
    forced = (j == 0) | (j == cur) | (j == cur - 1)
    work = jnp.where(forced, -jnp.inf, jnp.where(valid, imp_ref[0, 0, 0], -BIG))
    for _ in range(n_sel - 3):
        v, k = work, jf
        while v.shape[0] > 1:
            h = v.shape[0] // 2
            hi = v[h:] > v[:h]
            v = jnp.where(hi, v[h:], v[:h])
            k = jnp.where(hi, k[h:], k[:h])
        work = jnp.where(jf == k, -jnp.inf, work)
    selm = jnp.where(valid & (work == -jnp.inf), 1.0, 0.0)
    sel_ref[0, 0, 0] = selm

    nsb = ns // 2
    any_q = jnp.broadcast_to(jnp.max(selm, axis=2, keepdims=True), shape[:2] + (LIST_LANES,))
    any_q = any_q.reshape(ns * TOPK_TILES, LIST_LANES).astype(BF16)
    rows = lax.broadcasted_iota(jnp.int32, (nsb * TOPK_TILES, 1), 0)
    sb = jnp.right_shift(rows, 3)
    tile = c * TOPK_TILES + jnp.bitwise_and(rows, TOPK_TILES - 1)
    flag = jnp.dot(pair_ref[...], any_q, preferred_element_type=F32)
    flag = jnp.where((flag > 0.0) & (sb < tile), 1.0, 0.0)
    rank = jnp.dot(low_ref[...], flag.astype(BF16), preferred_element_type=F32)
    k_row = lax.broadcasted_iota(jnp.int32, (1, LIST_LANES), 1)
    hit = jnp.where((rank == (k_row + 1).astype(F32)) & (flag > 0.0), 1.0, 0.0).astype(BF16)
    lst = jnp.dot(pick_ref[...], hit, preferred_element_type=F32)
    total = rank[(nsb - 1) * TOPK_TILES:nsb * TOPK_TILES, :]
    lst_ref[0, 0, 0] = jnp.where(k_row == LIST_LANES - 1, total, lst).astype(jnp.int32)


def _topk(imp_t):
    b, kv, nc, ns, tiles, tq = imp_t.shape
    nsb = ns // 2
    r = np.arange(nsb * tiles)
    q = np.arange(ns * tiles)
    same = (r[:, None] % tiles) == (q[None, :] % tiles)
    pair = same & ((q[None, :] // tiles) // 2 == (r[:, None] // tiles))
    low = ((r[:, None] % tiles) == (r[None, :] % tiles)) & ((r[None, :] // tiles) <= (r[:, None] // tiles))
    pick = (np.arange(tiles)[:, None] == (r[None, :] % tiles)) * (r[None, :] // tiles)
    const = lambda a: jnp.asarray(a.astype(np.float32), dtype=BF16)
    whole = lambda a: pl.BlockSpec(a.shape, lambda bi, g, c: (0, 0))
    blk = pl.BlockSpec((1, 1, 1, ns, tiles, tq), lambda bi, g, c: (bi, g, c, 0, 0, 0))
    return pl.pallas_call(
        functools.partial(_topk_kernel, ns=ns, n_sel=min(SEL_TOPK, ns), tq=tq),
        grid=(b, kv, nc),
        in_specs=[blk, whole(pair), whole(low), whole(pick)],
        out_specs=[blk, pl.BlockSpec((1, 1, 1, tiles, LIST_LANES), lambda bi, g, c: (bi, g, c, 0, 0))],
        out_shape=[jax.ShapeDtypeStruct(imp_t.shape, F32),
                   jax.ShapeDtypeStruct((b, kv, nc, tiles, LIST_LANES), jnp.int32)],
        compiler_params=_cparams(("parallel", "parallel", "arbitrary")),
        name="topk",
    )(imp_t, const(pair), const(low), const(pick))


def _attn2_kernel(lst_ref, qT_ref, ks_ref, vsT_ref, sel_ref, gates_ref, ts_ref, srow_ref, part_ref,
                  o_ref, *, tq):
    n = NSA_GROUP * tq
    i0 = pl.program_id(2) * ATT_PAIR
    srow = srow_ref[0]
    tiles = [(t, i0 + t, jnp.bitwise_and(i0 + t, TOPK_TILES - 1)) for t in range(ATT_PAIR)]
    cnts = [lst_ref[0, 0, 0, a, LIST_LANES - 1] for _, _, a in tiles]

    def sel_rows(a, sb, base):
        out = []
        for half in range(2):
            row = jnp.where(sel_ref[0, 0, 0, 2 * sb + half, pl.ds(a, 1), :] > 0.0, 0.0, NEG)
            out.append(jnp.concatenate([row] * NSA_GROUP, axis=1) + base)
        return out

    def scores(t, sb, table):
        k0 = pl.multiple_of(sb * SUPER, SUPER)
        s = jnp.dot(ks_ref[0, 0, pl.ds(k0, SUPER), :], qT_ref[0, 0, t], preferred_element_type=F32) + table
        return s, vsT_ref[0, 0, :, pl.ds(k0, SUPER)]

    def listed(tile, k):
        t, i, a = tile
        sb = lst_ref[0, 0, 0, a, k]
        pen = jnp.where(k < cnts[t], 0.0, NEG)
        base = srow * (sb * SUPER - i * tq).astype(F32) + pen
        s, vT = scores(t, sb, ts_ref[0, 0])
        return s, sel_rows(a, sb, base), vT

    def diagonal(tile):
        t, i, a = tile
        s, vT = scores(t, i, ts_ref[0, 1])
        return s, sel_rows(a, i, 0.0), vT

    def group(first_k, with_diag):
        blocks_list = [[diagonal(tile)] if with_diag else [] for tile in tiles]
        for u in range(SEL_UNROLL - with_diag):
            for tile in tiles:
                blocks_list[tile[0]].append(listed(tile, first_k + u))
        return blocks_list

    states = _softmax_steps([_softmax_init(n)] * ATT_PAIR, group(0, 1))

    def body(it, sts):
        return tuple(_softmax_steps(list(sts), group(SEL_UNROLL - 1 + it * SEL_UNROLL, 0)))

    rest = functools.reduce(jnp.maximum, cnts) - (SEL_UNROLL - 1)
    n_it = (jnp.maximum(rest, 0) + SEL_UNROLL - 1) // SEL_UNROLL
    states = lax.fori_loop(0, n_it, body, tuple(states))
    for t, _, _ in tiles:
        _, l_s, acc_s = states[t]
        l_s = jnp.sum(l_s, axis=0, keepdims=True)
        gsig = jax.nn.sigmoid(gates_ref[0, 0, t])
        o_ref[0, 0, t] = (part_ref[0, 0, t] + gsig[1:2, :] * (acc_s * (1.0 / l_s))).astype(o_ref.dtype)


def _attn2(lst, qT, ks, vsT, sel_t, gatesT, t_s, srow, part):
    b, kv, nqt, dh, n = qT.shape
    tq = n // NSA_GROUP
    s = nqt * tq
    ns = s // SEL_BLOCK
    per_top = TOPK_TILES // ATT_PAIR
    tile = lambda r: pl.BlockSpec((1, 1, ATT_PAIR, r, n), lambda bi, g, i: (bi, g, i, 0, 0))
    per_bg = lambda r, c: pl.BlockSpec((1, 1, r, c), lambda bi, g, i: (bi, g, 0, 0))
    return pl.pallas_call(
        functools.partial(_attn2_kernel, tq=tq),
        grid=(b, kv, nqt // ATT_PAIR),
        in_specs=[pl.BlockSpec((1, 1, 1, TOPK_TILES, LIST_LANES), lambda bi, g, i: (bi, g, i // per_top, 0, 0),
                               memory_space=pltpu.SMEM),
                  tile(dh), per_bg(s, dh), per_bg(dh, s),
                  pl.BlockSpec((1, 1, 1, ns, TOPK_TILES, tq),
                               lambda bi, g, i: (bi, g, i // per_top, 0, 0, 0)),
                  tile(3),
                  pl.BlockSpec((1, 2, SUPER, n), lambda bi, g, i: (g, 0, 0, 0)),
                  pl.BlockSpec((1, 1, n), lambda bi, g, i: (g, 0, 0)),
                  tile(dh)],
        out_specs=tile(dh),
        out_shape=jax.ShapeDtypeStruct((b, kv, nqt, dh, n), BF16),
        compiler_params=_cparams(("parallel", "parallel", "arbitrary")),
        name="attn2",
    )(lst, qT, ks, vsT, sel_t, gatesT, t_s, srow, part)


def _alibi_slopes(n):
    return np.array([2.0 ** (-8.0 * (h + 1) / n) for h in range(n)], dtype=np.float32)


def _overlap_T(s):
    ncp = s // CMP_STRIDE
    nc = (s - CMP_LEN) // CMP_STRIDE + 1
    ns = s // SEL_BLOCK
    c_start = np.arange(ncp) * CMP_STRIDE
    s_start = np.arange(ns) * SEL_BLOCK
    ov = ((c_start[None, :] <= s_start[:, None] + SEL_BLOCK - 1) &
          (c_start[None, :] + CMP_LEN - 1 >= s_start[:, None]) & (np.arange(ncp)[None, :] < nc))
    return jnp.asarray(ov.astype(np.float32), dtype=BF16)


def _bias_tables(s):
    tq = ATT_TQ
    ncp = s // CMP_STRIDE
    slopes = jnp.asarray(_alibi_slopes(NSA_HEADS)).reshape(NSA_KV_HEADS, NSA_GROUP)
    srow = jnp.repeat(slopes, tq, axis=1)[:, None, :] * LOG2E
    t_rel = jnp.tile(jnp.arange(tq, dtype=jnp.int32), NSA_GROUP)[None, None, :]

    def table(rel_pos, lo, hi):
        dist = t_rel - rel_pos[None, :, None]
        bias = -srow * dist.astype(F32)
        return jnp.where((dist >= lo) & (dist < hi), bias, NEG)

    far = 1 << 30
    crel = (jnp.arange(2 * ncp, dtype=jnp.int32) - ncp) * CMP_STRIDE + (CMP_LEN - 1)
    t_c = table(crel, 0, far)
    t_w = table(jnp.arange(WINDOW + tq, dtype=jnp.int32) - WINDOW, 0, WINDOW)
    blk = jnp.arange(SUPER, dtype=jnp.int32)
    t_s = jnp.stack([table(blk, -far, far), table(blk, 0, far)], axis=1)
    return t_c, t_w, t_s, srow


def _nsa(q, kvs, ng, cmp_pos, w_ck1, w_ck2, w_cv1, w_cv2):
    b, s, _ = q.shape
    kv, grp, dh = NSA_KV_HEADS, NSA_GROUP, HEAD_DIM
    tq = ATT_TQ
    nqt = s // tq
    ns = s // SEL_BLOCK
    kc, vc, ks, vs, kw, vw = [kvs[..., i * KV_WIDTH:(i + 1) * KV_WIDTH] for i in range(6)]
    ckf = _compress(kc, cmp_pos, w_ck1, w_ck2)
    cvf = _compress(vc, cmp_pos, w_cv1, w_cv2)
    heads = lambda v: v.reshape(b, -1, kv, dh)
    rows = lambda v: heads(v).transpose(0, 2, 1, 3)
    cols = lambda v: heads(v).transpose(0, 2, 3, 1)
    tiles = lambda v, c: v.reshape(b, nqt, tq, kv, grp, c).transpose(0, 3, 1, 5, 4, 2).reshape(
        b, kv, nqt, c, grp * tq)
    qT = tiles(q, dh)
    gatesT = tiles(ng, 3)
    t_c, t_w, t_s, srow = _bias_tables(s)

    part, imp = _attn1(qT, rows(ckf), cols(cvf), rows(kw), cols(vw), gatesT, t_c, t_w, _overlap_T(s))
    imp_t = imp.reshape(b, kv, nqt // TOPK_TILES, TOPK_TILES, ns, tq).transpose(0, 1, 2, 4, 3, 5)
    sel_t, lst = _topk(imp_t)
    y = _attn2(lst, qT, rows(ks), cols(vs), sel_t, gatesT, t_s, srow, part)
    y = y.reshape(b, kv, nqt, dh, grp, tq).transpose(0, 2, 5, 1, 4, 3)
    return y.reshape(b, s, NSA_WIDTH)


HALO = 8
CONV_SLAB = 512


def _softplus(x):
    return jnp.maximum(x, 0.0) + jnp.log1p(jnp.exp(-jnp.abs(x)))


def _ssd_kernel(raw_ref, z_ref, dt_ref, dtT_ref, cw_ref, cb_ref, dtb_ref, dtbT_ref,
                al_ref, alT_ref, dsk_ref, nw_ref, o_ref, st_ref, tail_ref, ext_ref, xc_ref, y_ref):
    q = SSM_CHUNK

    @pl.when(pl.program_id(1) == 0)
    def _():
        st_ref[...] = jnp.zeros(st_ref.shape, F32)
        tail_ref[...] = jnp.zeros(tail_ref.shape, F32)

    ext_ref[0:HALO, :] = tail_ref[...]
    ext_ref[HALO:HALO + q, :] = raw_ref[0].astype(F32)
    tail_ref[...] = ext_ref[q:q + HALO, :]
    for c0 in range(0, CONV_DIM, CONV_SLAB):
        sl = slice(c0, c0 + CONV_SLAB)
        y = cb_ref[:, sl]
        for k in range(CONV_WIDTH):
            off = HALO - (CONV_WIDTH - 1) + k
            y = y + cw_ref[k:k + 1, sl] * ext_ref[off:off + q, sl]
        xc_ref[:, sl] = y * jax.nn.sigmoid(y)

    dt = _softplus(dt_ref[0][:, :SSM_HEADS] + dtb_ref[...])
    dtT = _softplus(dtT_ref[0] + dtbT_ref[...])
    ri = lax.broadcasted_iota(jnp.int32, (q, q), 0)
    ci = lax.broadcasted_iota(jnp.int32, (q, q), 1)
    tri = ri >= ci
    low = jnp.where(tri, 1.0, 0.0).astype(BF16)
    upp = jnp.where(ri <= ci, 1.0, 0.0).astype(BF16)
    cum = _dot01_l(low, dt * (-jnp.exp(al_ref[...])))
    cumT = _dot01_r(dtT * (-jnp.exp(alT_ref[...])), upp)
    ecum = jnp.exp(cum)
    lane_lo = lax.broadcasted_iota(jnp.int32, (1, LANES), 1) < SSM_HEAD_DIM
    b0 = SSM_INNER
    c0 = SSM_INNER + SSM_GROUPS * SSM_STATE

    for g in range(SSM_GROUPS):
        bgf = xc_ref[:, b0 + g * SSM_STATE:b0 + (g + 1) * SSM_STATE]
        cg = xc_ref[:, c0 + g * SSM_STATE:c0 + (g + 1) * SSM_STATE].astype(BF16)
        cb = lax.dot_general(cg, bgf.astype(BF16), (((1,), (1,)), ((), ())), preferred_element_type=F32)
        bTg = bgf.T
        for pr in range(SSM_HPG // 2):
            hp = g * (SSM_HPG // 2) + pr
            xp = xc_ref[:, hp * LANES:(hp + 1) * LANES].astype(BF16)
            ys, sts, ecs, els = [], [], [], []
            for k in range(2):
                h = 2 * hp + k
                crow = cumT[h:h + 1, :]
                seg = cum[:, h:h + 1] - crow
                w = cb * jnp.exp(jnp.where(tri, seg, NEG)) * dtT[h:h + 1, :]
                ys.append(jnp.dot(w.astype(BF16), xp, preferred_element_type=F32))
                clast = crow[:, q - 1:q]
                to_end = jnp.exp(clast - crow) * dtT[h:h + 1, :]
                sts.append(jnp.dot((bTg * to_end).astype(BF16), xp, preferred_element_type=F32))
                ecs.append(ecum[:, h:h + 1])
                els.append(jnp.exp(clast))
            st = st_ref[hp]
            y_in = jnp.dot(cg, st.astype(BF16), preferred_element_type=F32)
            y_ref[:, hp * LANES:(hp + 1) * LANES] = (
                jnp.where(lane_lo, ys[0], ys[1]) + y_in * jnp.where(lane_lo, ecs[0], ecs[1]))
            st_ref[hp] = st * jnp.where(lane_lo, els[0], els[1]) + jnp.where(lane_lo, sts[0], sts[1])

    gw = SSM_INNER // SSM_GROUPS
    for g in range(SSM_GROUPS):
        sl = slice(g * gw, (g + 1) * gw)
        z = z_ref[0][:, sl].astype(F32)
        y = (y_ref[:, sl] + dsk_ref[:, sl] * xc_ref[:, sl]) * (z * jax.nn.sigmoid(z))
        y = y * lax.rsqrt(jnp.mean(y * y, axis=-1, keepdims=True) + EPS)
        o_ref[0, :, sl] = (y * nw_ref[:, sl]).astype(o_ref.dtype)


def _ssd(xbc, z, dts, conv_w, conv_b, dt_bias, a_log, d_skip, ssm_norm):
    b, s, _ = xbc.shape
    q = SSM_CHUNK
    hh = SSM_HEADS
    dtT = dts[..., :hh].transpose(0, 2, 1)
    row = lambda v: v.reshape(1, -1).astype(F32)
    col = lambda v: v.reshape(-1, 1).astype(F32)
    const = lambda shape: pl.BlockSpec(shape, lambda bi, c: (0, 0))
    return pl.pallas_call(
        _ssd_kernel,
        grid=(b, s // q),
        in_specs=[pl.BlockSpec((1, q, CONV_DIM), lambda bi, c: (bi, c, 0)),
                  pl.BlockSpec((1, q, SSM_INNER), lambda bi, c: (bi, c, 0)),
                  pl.BlockSpec((1, q, LANES), lambda bi, c: (bi, c, 0)),
                  pl.BlockSpec((1, hh, q), lambda bi, c: (bi, 0, c)),
                  const((CONV_WIDTH, CONV_DIM)), const((1, CONV_DIM)),
                  const((1, hh)), const((hh, 1)), const((1, hh)), const((hh, 1)),
                  const((1, SSM_INNER)), const((1, SSM_INNER))],
        out_specs=pl.BlockSpec((1, q, SSM_INNER), lambda bi, c: (bi, c, 0)),
        out_shape=jax.ShapeDtypeStruct((b, s, SSM_INNER), BF16),
        scratch_shapes=[pltpu.VMEM((hh // 2, SSM_STATE, LANES), F32),
                        pltpu.VMEM((HALO, CONV_DIM), F32),
                        pltpu.VMEM((q + HALO, CONV_DIM), F32),
                        pltpu.VMEM((q, CONV_DIM), F32),
                        pltpu.VMEM((q, SSM_INNER), F32)],
        compiler_params=_cparams(("parallel", "arbitrary")),
        name="ssd",
    )(xbc, z, dts, dtT, conv_w, row(conv_b), row(dt_bias), col(dt_bias), row(a_log), col(a_log),
      row(jnp.repeat(d_skip, SSM_HEAD_DIM)), row(ssm_norm))


MERGE_TM = 512


def _merge_kernel(h_ref, ya_ref, ym_ref, mg_ref, wa_ref, ws_ref, wo_ref, post_ref, o_ref):
    d = h_ref.shape[1]
    a = jnp.dot(ya_ref[...], wa_ref[...], preferred_element_type=F32)
    m = jnp.dot(ym_ref[...], ws_ref[...], preferred_element_type=F32)
    gts = jax.nn.sigmoid(mg_ref[...].astype(F32))
    merged = gts[:, :d] * a + gts[:, d:] * m
    out = jnp.dot(merged.astype(BF16), wo_ref[...], preferred_element_type=F32)
    o_ref[...] = h_ref[...] + _rms(out, post_ref[...])


def _merge(h2d, ya, ym, mg, wa, ws, wo, post):
    t, d = h2d.shape
    tm = min(MERGE_TM, t)
    const = lambda shape: pl.BlockSpec(shape, lambda i: (0, 0), pipeline_mode=pl.Buffered(1))
    rows = lambda n: pl.BlockSpec((tm, n), lambda i: (i, 0))
    return pl.pallas_call(
        _merge_kernel,
        grid=(t // tm,),
        in_specs=[rows(d), rows(NSA_WIDTH), rows(SSM_INNER), rows(2 * d),
                  const((NSA_WIDTH, d)), const((SSM_INNER, d)), const((d, d)), const((1, d))],
        out_specs=rows(d),
        out_shape=jax.ShapeDtypeStruct((t, d), F32),
        compiler_params=_cparams(("parallel",)),
        name="merge",
    )(h2d, ya, ym, mg, wa.astype(BF16), ws.astype(BF16), wo.astype(BF16), post.reshape(1, d))


def _mixer(h2d, b, s, mix_pre, w_in, cmp_pos, w_ck1, w_ck2, w_cv1, w_cv2, conv_w, conv_b, dt_bias,
           a_log, d_skip, ssm_norm, w_attn_branch, w_ssm_branch, w_out, mix_post):
    o = 0
    cols = {}
    for name, width in (("q", NSA_WIDTH), ("kv", 6 * KV_WIDTH), ("ng", NSA_HEADS * 3), ("z", SSM_INNER),
                        ("xbc", CONV_DIM), ("dt", SSM_HEADS), ("mg", 2 * D_MODEL)):
        cols[name] = w_in[:, o:o + width]
        o += width
    pad = LANES - SSM_HEADS - NSA_HEADS * 3
    w_small = jnp.concatenate([cols["dt"], cols["ng"], jnp.zeros((D_MODEL, pad), F32)], axis=1)

    u = _rmsnorm(h2d, mix_pre)
    q = _proj(u, cols["q"], BF16, scale=HEAD_DIM ** -0.5 * LOG2E).reshape(b, s, NSA_WIDTH)
    kvs = _proj(u, cols["kv"], BF16).reshape(b, s, 6 * KV_WIDTH)
    z = _proj(u, cols["z"], BF16).reshape(b, s, SSM_INNER)
    xbc = _proj(u, cols["xbc"], BF16).reshape(b, s, CONV_DIM)
    mg = _proj(u, cols["mg"], BF16)
    small = _proj(u, w_small, F32).reshape(b, s, LANES)
    ng = small[..., SSM_HEADS:SSM_HEADS + NSA_HEADS * 3]

    y_a = _nsa(q, kvs, ng, cmp_pos, w_ck1, w_ck2, w_cv1, w_cv2)
    y_m = _ssd(xbc, z, small, conv_w, conv_b, dt_bias, a_log, d_skip, ssm_norm)
    return _merge(h2d, y_a.reshape(b * s, NSA_WIDTH), y_m.reshape(b * s, SSM_INNER), mg,
                  w_attn_branch, w_ssm_branch, w_out, mix_post)


def kernel(x, ffn1_pre, ffn1_gate, ffn1_up, ffn1_down, ffn1_post, mix_pre, w_in, cmp_pos, w_ck1, w_ck2,
           w_cv1, w_cv2, conv_w, conv_b, dt_bias, a_log, d_skip, ssm_norm, w_attn_branch, w_ssm_branch,
           w_out, mix_post, ffn2_pre, ffn2_gate, ffn2_up, ffn2_down, ffn2_post):
    b, s, d = x.shape
    h = x.reshape(b * s, d)
    for l in range(ffn1_pre.shape[0]):
        h = _ffn(h, ffn1_pre[l], ffn1_gate[l], ffn1_up[l], ffn1_down[l], ffn1_post[l])
        h = _mixer(h, b, s, mix_pre[l], w_in[l], cmp_pos[l], w_ck1[l], w_ck2[l], w_cv1[l], w_cv2[l],
                   conv_w[l], conv_b[l], dt_bias[l], a_log[l], d_skip[l], ssm_norm[l],
                   w_attn_branch[l], w_ssm_branch[l], w_out[l], mix_post[l])
        h = _ffn(h, ffn2_pre[l], ffn2_gate[l], ffn2_up[l], ffn2_down[l], ffn2_post[l])
    return h.reshape(b, s, d)
```

```python
import functools

import numpy as np
import jax
import jax.numpy as jnp
from jax import lax
from jax.experimental import pallas as pl
from jax.experimental.pallas import tpu as pltpu

F32 = jnp.float32
BF16 = jnp.bfloat16

D_MODEL = 1024
NSA_HEADS = 16
NSA_KV_HEADS = 4
NSA_GROUP = NSA_HEADS // NSA_KV_HEADS
HEAD_DIM = 64
CMP_LEN = 32
CMP_STRIDE = 16
CMP_HIDDEN = 2 * HEAD_DIM
SEL_BLOCK = 64
SEL_TOPK = 16
WINDOW = 512
NSA_WIDTH = NSA_HEADS * HEAD_DIM
KV_WIDTH = NSA_KV_HEADS * HEAD_DIM
BIG = 1e4
NEG = -1e30

SSM_INNER = 2 * D_MODEL
SSM_HEAD_DIM = 64
SSM_HEADS = SSM_INNER // SSM_HEAD_DIM
SSM_GROUPS = 4
SSM_HPG = SSM_HEADS // SSM_GROUPS
SSM_STATE = 128
CONV_WIDTH = 4
SSM_CHUNK = 256
CONV_DIM = SSM_INNER + 2 * SSM_GROUPS * SSM_STATE
FFN_HIDDEN = 2816
EPS = 1e-6
LOG2E = 1.4426950408889634

LANES = 128
VMEM_LIMIT = 56 * 1024 * 1024


def _cparams(sem):
    return pltpu.CompilerParams(dimension_semantics=sem, vmem_limit_bytes=VMEM_LIMIT)


def _rms(x, g):
    return x * lax.rsqrt(jnp.mean(x * x, axis=-1, keepdims=True) + EPS) * g


def _split3(x):
    hi = x.astype(BF16)
    r1 = x - hi.astype(F32)
    mid = r1.astype(BF16)
    lo = (r1 - mid.astype(F32)).astype(BF16)
    return hi, mid, lo


def _dot01_l(a01, x):
    hi, mid, lo = _split3(x)
    d = lambda v: jnp.dot(a01, v, preferred_element_type=F32)
    return d(hi) + (d(mid) + d(lo))


def _dot01_r(x, a01):
    hi, mid, lo = _split3(x)
    d = lambda v: jnp.dot(v, a01, preferred_element_type=F32)
    return d(hi) + (d(mid) + d(lo))


def _interleave(gens):
    live = list(gens)
    while live:
        live = [g for g in live if next(g, StopIteration) is not StopIteration]


FFN_TM = 256
FFN_HC = 256
FFN_TILES = 4


def _ffn_tile(t, x_ref, pre_ref, wg_ref, wu_ref, wd_ref, post_ref, o_ref):
    rows = pl.ds(t * FFN_TM, FFN_TM)
    x = x_ref[rows, :]
    ub = _rms(x, pre_ref[...]).astype(BF16)
    yield
    acc = jnp.zeros(x.shape, F32)
    for c in range(FFN_HIDDEN // FFN_HC):
        sl = slice(c * FFN_HC, (c + 1) * FFN_HC)
        g = jnp.dot(ub, wg_ref[:, sl], preferred_element_type=F32)
        u = jnp.dot(ub, wu_ref[:, sl], preferred_element_type=F32)
        a = (g * jax.nn.sigmoid(g)) * u
        acc = acc + jnp.dot(a.astype(BF16), wd_ref[sl, :], preferred_element_type=F32)
        yield
    o_ref[rows, :] = x + 0.5 * _rms(acc, post_ref[...])
    yield


def _ffn_kernel(*refs):
    _interleave([_ffn_tile(t, *refs) for t in range(FFN_TILES)])


def _ffn(h2d, pre, wg, wu, wd, post):
    t, d = h2d.shape
    tm = min(FFN_TM * FFN_TILES, t)
    const = lambda shape: pl.BlockSpec(shape, lambda i: (0, 0), pipeline_mode=pl.Buffered(1))
    return pl.pallas_call(
        _ffn_kernel,
        grid=(t // tm,),
        in_specs=[pl.BlockSpec((tm, d), lambda i: (i, 0)),
                  const((1, d)), const((d, FFN_HIDDEN)), const((d, FFN_HIDDEN)),
                  const((FFN_HIDDEN, d)), const((1, d))],
        out_specs=pl.BlockSpec((tm, d), lambda i: (i, 0)),
        out_shape=jax.ShapeDtypeStruct((t, d), F32),
        compiler_params=_cparams(("parallel",)),
        name="ffn",
    )(h2d, pre.reshape(1, d), wg.astype(BF16), wu.astype(BF16), wd.astype(BF16), post.reshape(1, d))


PROJ_TM = 1024
PROJ_TN = 1024


def _rmsnorm_kernel(x_ref, g_ref, o_ref):
    o_ref[...] = _rms(x_ref[...], g_ref[...]).astype(o_ref.dtype)


def _rmsnorm(h2d, g):
    t, d = h2d.shape
    tm = min(PROJ_TM, t)
    return pl.pallas_call(
        _rmsnorm_kernel,
        grid=(t // tm,),
        in_specs=[pl.BlockSpec((tm, d), lambda i: (i, 0)), pl.BlockSpec((1, d), lambda i: (0, 0))],
        out_specs=pl.BlockSpec((tm, d), lambda i: (i, 0)),
        out_shape=jax.ShapeDtypeStruct((t, d), BF16),
        compiler_params=_cparams(("parallel",)),
        name="rmsnorm",
    )(h2d, g.reshape(1, d))


def _proj_kernel(u_ref, w_ref, o_ref, *, scale):
    acc = jnp.dot(u_ref[...], w_ref[...], preferred_element_type=F32)
    if scale != 1.0:
        acc = acc * scale
    o_ref[...] = acc.astype(o_ref.dtype)


def _proj(u, w, out_dtype, scale=1.0):
    t, d = u.shape
    n = w.shape[1]
    tm = min(PROJ_TM, t)
    tn = next(c for c in range(min(PROJ_TN, n), 0, -LANES) if n % c == 0)
    return pl.pallas_call(
        functools.partial(_proj_kernel, scale=scale),
        grid=(t // tm, n // tn),
        in_specs=[pl.BlockSpec((tm, d), lambda i, j: (i, 0)),
                  pl.BlockSpec((d, tn), lambda i, j: (0, j))],
        out_specs=pl.BlockSpec((tm, tn), lambda i, j: (i, j)),
        out_shape=jax.ShapeDtypeStruct((t, n), out_dtype),
        compiler_params=_cparams(("parallel", "parallel")),
        name="proj",
    )(u, w.astype(BF16))


def _proj_q_kernel(u_ref, wT_ref, o_ref, *, scale, tq):
    r = lax.dot_general(wT_ref[...], u_ref[...], (((1,), (1,)), ((), ())), preferred_element_type=F32)
    r = (r * scale).astype(o_ref.dtype)
    for t in range(u_ref.shape[0] // tq):
        for h in range(NSA_GROUP):
            o_ref[0, 0, t, :, h * tq:(h + 1) * tq] = r[h * HEAD_DIM:(h + 1) * HEAD_DIM, t * tq:(t + 1) * tq]


def _proj_q(u, w, b, s, scale, tq):
    t, d = u.shape
    tm = min(PROJ_TM, s)
    per_seq = s // tm
    gw = NSA_GROUP * HEAD_DIM
    return pl.pallas_call(
        functools.partial(_proj_q_kernel, scale=scale, tq=tq),
        grid=(t // tm, NSA_KV_HEADS),
        in_specs=[pl.BlockSpec((tm, d), lambda i, g: (i, 0)),
                  pl.BlockSpec((gw, d), lambda i, g: (g, 0))],
        out_specs=pl.BlockSpec((1, 1, tm // tq, HEAD_DIM, NSA_GROUP * tq),
                               lambda i, g: (i // per_seq, g, i % per_seq, 0, 0)),
        out_shape=jax.ShapeDtypeStruct((b, NSA_KV_HEADS, s // tq, HEAD_DIM, NSA_GROUP * tq), BF16),
        compiler_params=_cparams(("parallel", "parallel")),
        name="proj_q",
    )(u, w.T.astype(BF16))


def _compress_kernel(r_ref, pt_ref, pb_ref, w1t_ref, w1b_ref, w2_ref, o_ref):
    r = r_ref[0].astype(F32)
    top = jnp.dot((r + pt_ref[...]).astype(BF16), w1t_ref[...], preferred_element_type=F32)
    bot = jnp.dot((r + pb_ref[...]).astype(BF16), w1b_ref[...], preferred_element_type=F32)
    ncp = r.shape[0]
    h = top + pltpu.roll(bot, ncp - 1, 0)
    a = (h * jax.nn.sigmoid(h)).astype(BF16)
    o_ref[0] = jnp.dot(a, w2_ref[...], preferred_element_type=F32).astype(o_ref.dtype)


def _compress(k, pos, w1, w2):
    b, s, _ = k.shape
    ncp = s // CMP_STRIDE
    half = CMP_LEN // 2
    rw = half * KV_WIDTH
    eye = jnp.eye(NSA_KV_HEADS, dtype=F32)
    w1r = w1.reshape(CMP_LEN, HEAD_DIM, CMP_HIDDEN)

    def big(wpart):
        return jnp.einsum('ldj,gk->lgdkj', wpart, eye).reshape(rw, NSA_KV_HEADS * CMP_HIDDEN).astype(BF16)

    def posrow(p):
        return jnp.broadcast_to(p[:, None, :], (half, NSA_KV_HEADS, HEAD_DIM)).reshape(1, rw)

    w2big = jnp.einsum('jd,gk->gjkd', w2, eye).reshape(NSA_KV_HEADS * CMP_HIDDEN, KV_WIDTH).astype(BF16)
    nh = NSA_KV_HEADS * CMP_HIDDEN
    const = lambda shape: pl.BlockSpec(shape, lambda i: (0, 0))
    return pl.pallas_call(
        _compress_kernel,
        grid=(b,),
        in_specs=[pl.BlockSpec((1, ncp, rw), lambda i: (i, 0, 0)),
                  const((1, rw)), const((1, rw)), const((rw, nh)), const((rw, nh)),
                  const((nh, KV_WIDTH))],
        out_specs=pl.BlockSpec((1, ncp, KV_WIDTH), lambda i: (i, 0, 0)),
        out_shape=jax.ShapeDtypeStruct((b, ncp, KV_WIDTH), BF16),
        compiler_params=_cparams(("parallel",)),
        name="compress",
    )(k.reshape(b, ncp, rw), posrow(pos[:half]), posrow(pos[half:]), big(w1r[:half]), big(w1r[half:]), w2big)


ATT_TQ = 128
SUPER = 2 * SEL_BLOCK
SEL_UNROLL = 5
LIST_LANES = LANES
TOPK_TILES = 8
ATT_PAIR = 8


def _softmax_steps(states, blocks_list):
    n = states[0][0].shape[1]
    ms, ls, accs = [], [], []
    for (m, l8, acc), blocks in zip(states, blocks_list):
        part = None
        for s, rows, _ in blocks:
            hr = s.shape[0] // len(rows)
            for a, r in enumerate(rows):
                c = jnp.max(s[a * hr:(a + 1) * hr].reshape(hr // 8, 8, n), axis=0) + r
                part = c if part is None else jnp.maximum(part, c)
        m_new = jnp.maximum(m, jnp.max(part, axis=0, keepdims=True))
        alpha = jnp.exp2(m - m_new)
        ms.append(m_new)
        ls.append(l8 * alpha)
        accs.append(acc * alpha)
    for u in range(max(len(blocks) for blocks in blocks_list)):
        for k, blocks in enumerate(blocks_list):
            if u >= len(blocks):
                continue
            s, rows, vT = blocks[u]
            hr = s.shape[0] // len(rows)
            ps = [jnp.exp2(s[a * hr:(a + 1) * hr] - (ms[k] - r)) for a, r in enumerate(rows)]
            p = ps[0] if len(ps) == 1 else jnp.concatenate(ps, axis=0)
            ls[k] = ls[k] + jnp.sum(p.reshape(p.shape[0] // 8, 8, n), axis=0)
            accs[k] = accs[k] + jnp.dot(vT, p.astype(BF16), preferred_element_type=F32)
    return [(ms[k], ls[k], accs[k]) for k in range(len(states))]


def _softmax_step(state, blocks):
    return _softmax_steps([state], [blocks])[0]


def _softmax_init(n):
    return (jnp.full((1, n), NEG, F32), jnp.zeros((8, n), F32), jnp.zeros((HEAD_DIM, n), F32))


def _attn1_tile(t, i, qT_ref, ck_ref, cvT_ref, kw_ref, vwT_ref, gates_ref, tc_ref, tw_ref, ovT_ref,
                part_ref, imp_ref, tq, ncp):
    q0 = i * tq
    n = NSA_GROUP * tq
    qT = qT_ref[0, 0, t]

    n_kt = WINDOW // tq + 1
    blocks = []
    for kt in range(n_kt):
        k0 = q0 - WINDOW + kt * tq
        pen = jnp.where(k0 < 0, NEG, 0.0)
        k0 = pl.multiple_of(jnp.maximum(k0, 0), tq)
        sw = jnp.dot(kw_ref[0, 0, pl.ds(k0, tq), :], qT, preferred_element_type=F32) + \
            tw_ref[0, kt * tq:(kt + 1) * tq, :]
        blocks.append((sw, [pen], vwT_ref[0, 0, :, pl.ds(k0, tq)]))
    yield

    start = pl.multiple_of(ncp - i * (tq // CMP_STRIDE), 8)
    s = jnp.dot(ck_ref[0, 0], qT, preferred_element_type=F32) + tc_ref[0, pl.ds(start, ncp), :]
    yield
    e = jnp.exp2(s - jnp.max(s, axis=0, keepdims=True))
    yield
    t_rel = jnp.bitwise_and(lax.broadcasted_iota(jnp.int32, (1, n), 1), tq - 1)
    inv = jnp.where(q0 + t_rel >= CMP_LEN - 1, 1.0 / jnp.sum(e, axis=0, keepdims=True), 0.0)
    p = e * inv
    o_c = jnp.dot(cvT_ref[0, 0], p.astype(BF16), preferred_element_type=F32)
    yield
    psum = p[:, 0:tq]
    for h in range(1, NSA_GROUP):
        psum = psum + p[:, h * tq:(h + 1) * tq]
    imp_ref[0, 0, t] = _dot01_l(ovT_ref[...], psum)
    yield

    _, l_w, acc_w = _softmax_step(_softmax_init(n), blocks)
    l_w = jnp.sum(l_w, axis=0, keepdims=True)
    yield

    gsig = jax.nn.sigmoid(gates_ref[0, 0, t])
    part_ref[0, 0, t] = gsig[0:1, :] * o_c + gsig[2:3, :] * (acc_w * (1.0 / l_w))
    yield


def _attn1_kernel(*refs, tq, ncp):
    i0 = pl.program_id(2) * ATT_PAIR
    _interleave([_attn1_tile(t, i0 + t, *refs, tq, ncp) for t in range(ATT_PAIR)])


def _attn1(qT, ck, cvT, kw, vwT, gatesT, t_c, t_w, ovT):
    b, kv, nqt, dh, n = qT.shape
    tq = n // NSA_GROUP
    s = nqt * tq
    ncp = ck.shape[2]
    ns = s // SEL_BLOCK
    kern = functools.partial(_attn1_kernel, tq=tq, ncp=ncp)
    tile = lambda r, c: pl.BlockSpec((1, 1, ATT_PAIR, r, c), lambda bi, g, i: (bi, g, i, 0, 0))
    per_bg = lambda r, c: pl.BlockSpec((1, 1, r, c), lambda bi, g, i: (bi, g, 0, 0))
    per_g = lambda r: pl.BlockSpec((1, r, n), lambda bi, g, i: (g, 0, 0))
    return pl.pallas_call(
        kern,
        grid=(b, kv, nqt // ATT_PAIR),
        in_specs=[tile(dh, n), per_bg(ncp, dh), per_bg(dh, ncp), per_bg(s, dh), per_bg(dh, s), tile(3, n),
                  per_g(2 * ncp), per_g(WINDOW + tq),
                  pl.BlockSpec((ns, ncp), lambda bi, g, i: (0, 0))],
        out_specs=[tile(dh, n), tile(ns, tq)],
        out_shape=[jax.ShapeDtypeStruct((b, kv, nqt, dh, n), F32),
                   jax.ShapeDtypeStruct((b, kv, nqt, ns, tq), F32)],
        compiler_params=_cparams(("parallel", "parallel", "arbitrary")),
        name="attn1",
    )(qT, ck, cvT, kw, vwT, gatesT, t_c, t_w, ovT)


def _topk_kernel(imp_ref, pair_ref, low_ref, pick_ref, sel_ref, lst_ref, *, ns, n_sel, tq):
    c = pl.program_id(2)
    shape = (ns, TOPK_TILES, tq)
    j = lax.broadcasted_iota(jnp.int32, shape, 0)
    jf = j.astype(F32)
    t = (c * TOPK_TILES + lax.broadcasted_iota(jnp.int32, shape, 1)) * tq + \
        lax.broadcasted_iota(jnp.int32, shape, 2)
    cur = jnp.right_shift(t, SEL_BLOCK.bit_length() - 1)
    valid = j <= cur
    forced = (j == 0) | (j == cur) | (j == cur - 1)
    work = jnp.where(forced, -jnp.inf, jnp.where(valid, imp_ref[0, 0, 0], -BIG))
    for _ in range(n_sel - 3):
        v, k = work, jf
        while v.shape[0] > 1:
            h = v.shape[0] // 2
            hi = v[h:] > v[:h]
            v = jnp.where(hi, v[h:], v[:h])
            k = jnp.where(hi, k[h:], k[:h])
        work = jnp.where(jf == k, -jnp.inf, work)
    selm = jnp.where(valid & (work == -jnp.inf), 1.0, 0.0)
    sel_ref[0, 0, 0] = selm

    nsb = ns // 2
    any_q = jnp.broadcast_to(jnp.max(selm, axis=2, keepdims=True), shape[:2] + (LIST_LANES,))
    any_q = any_q.reshape(ns * TOPK_TILES, LIST_LANES).astype(BF16)
    rows = lax.broadcasted_iota(jnp.int32, (nsb * TOPK_TILES, 1), 0)
    sb = jnp.right_shift(rows, 3)
    tile = c * TOPK_TILES + jnp.bitwise_and(rows, TOPK_TILES - 1)
    flag = jnp.dot(pair_ref[...], any_q, preferred_element_type=F32)
    flag = jnp.where((flag > 0.0) & (sb < tile), 1.0, 0.0)
    rank = jnp.dot(low_ref[...], flag.astype(BF16), preferred_element_type=F32)
    k_row = lax.broadcasted_iota(jnp.int32, (1, LIST_LANES), 1)
    hit = jnp.where((rank == (k_row + 1).astype(F32)) & (flag > 0.0), 1.0, 0.0).astype(BF16)
    lst = jnp.dot(pick_ref[...], hit, preferred_element_type=F32)
    total = rank[(nsb - 1) * TOPK_TILES:nsb * TOPK_TILES, :]
    lst_ref[0, 0, 0] = jnp.where(k_row == LIST_LANES - 1, total, lst).astype(jnp.int32)


def _topk(imp_t):
    b, kv, nc, ns, tiles, tq = imp_t.shape
    nsb = ns // 2
    r = np.arange(nsb * tiles)
    q = np.arange(ns * tiles)
    same = (r[:, None] % tiles) == (q[None, :] % tiles)
    pair = same & ((q[None, :] // tiles) // 2 == (r[:, None] // tiles))
    low = ((r[:, None] % tiles) == (r[None, :] % tiles)) & ((r[None, :] // tiles) <= (r[:, None] // tiles))
    pick = (np.arange(tiles)[:, None] == (r[None, :] % tiles)) * (r[None, :] // tiles)
    const = lambda a: jnp.asarray(a.astype(np.float32), dtype=BF16)
    whole = lambda a: pl.BlockSpec(a.shape, lambda bi, g, c: (0, 0))
    blk = pl.BlockSpec((1, 1, 1, ns, tiles, tq), lambda bi, g, c: (bi, g, c, 0, 0, 0))
    return pl.pallas_call(
        functools.partial(_topk_kernel, ns=ns, n_sel=min(SEL_TOPK, ns), tq=tq),
        grid=(b, kv, nc),
        in_specs=[blk, whole(pair), whole(low), whole(pick)],
        out_specs=[blk, pl.BlockSpec((1, 1, 1, tiles, LIST_LANES), lambda bi, g, c: (bi, g, c, 0, 0))],
        out_shape=[jax.ShapeDtypeStruct(imp_t.shape, F32),
                   jax.ShapeDtypeStruct((b, kv, nc, tiles, LIST_LANES), jnp.int32)],
        compiler_params=_cparams(("parallel", "parallel", "arbitrary")),
        name="topk",
    )(imp_t, const(pair), const(low), const(pick))


def _attn2_kernel(lst_ref, qT_ref, ks_ref, vsT_ref, sel_ref, gates_ref, ts_ref, srow_ref, part_ref,
                  o_ref, *, tq):
    n = NSA_GROUP * tq
    i0 = pl.program_id(2) * ATT_PAIR
    srow = srow_ref[0]
    tiles = [(t, i0 + t, jnp.bitwise_and(i0 + t, TOPK_TILES - 1)) for t in range(ATT_PAIR)]
    cnts = [lst_ref[0, 0, 0, a, LIST_LANES - 1] for _, _, a in tiles]

    def sel_rows(a, sb, base):
        out = []
        for half in range(2):
            row = jnp.where(sel_ref[0, 0, 0, 2 * sb + half, pl.ds(a, 1), :] > 0.0, 0.0, NEG)
            out.append(jnp.concatenate([row] * NSA_GROUP, axis=1) + base)
        return out

    def scores(t, sb, table):
        k0 = pl.multiple_of(sb * SUPER, SUPER)
        s = jnp.dot(ks_ref[0, 0, pl.ds(k0, SUPER), :], qT_ref[0, 0, t], preferred_element_type=F32) + table
        return s, vsT_ref[0, 0, :, pl.ds(k0, SUPER)]

    def listed(tile, k):
        t, i, a = tile
        sb = lst_ref[0, 0, 0, a, k]
        pen = jnp.where(k < cnts[t], 0.0, NEG)
        base = srow * (sb * SUPER - i * tq).astype(F32) + pen
        s, vT = scores(t, sb, ts_ref[0, 0])
        return s, sel_rows(a, sb, base), vT

    def diagonal(tile):
        t, i, a = tile
        s, vT = scores(t, i, ts_ref[0, 1])
        return s, sel_rows(a, i, 0.0), vT

    def group(first_k, with_diag):
        blocks_list = [[diagonal(tile)] if with_diag else [] for tile in tiles]
        for u in range(SEL_UNROLL - with_diag):
            for tile in tiles:
                blocks_list[tile[0]].append(listed(tile, first_k + u))
        return blocks_list

    states = _softmax_steps([_softmax_init(n)] * ATT_PAIR, group(0, 1))

    def body(it, sts):
        return tuple(_softmax_steps(list(sts), group(SEL_UNROLL - 1 + it * SEL_UNROLL, 0)))

    rest = functools.reduce(jnp.maximum, cnts) - (SEL_UNROLL - 1)
    n_it = (jnp.maximum(rest, 0) + SEL_UNROLL - 1) // SEL_UNROLL
    states = lax.fori_loop(0, n_it, body, tuple(states))
    for t, _, _ in tiles:
        _, l_s, acc_s = states[t]
        l_s = jnp.sum(l_s, axis=0, keepdims=True)
        gsig = jax.nn.sigmoid(gates_ref[0, 0, t])
        o_ref[0, 0, t] = (part_ref[0, 0, t] + gsig[1:2, :] * (acc_s * (1.0 / l_s))).astype(o_ref.dtype)


def _attn2(lst, qT, ks, vsT, sel_t, gatesT, t_s, srow, part):
    b, kv, nqt, dh, n = qT.shape
    tq = n // NSA_GROUP
    s = nqt * tq
    ns = s // SEL_BLOCK
    per_top = TOPK_TILES // ATT_PAIR
    tile = lambda r: pl.BlockSpec((1, 1, ATT_PAIR, r, n), lambda bi, g, i: (bi, g, i, 0, 0))
    per_bg = lambda r, c: pl.BlockSpec((1, 1, r, c), lambda bi, g, i: (bi, g, 0, 0))
    return pl.pallas_call(
        functools.partial(_attn2_kernel, tq=tq),
        grid=(b, kv, nqt // ATT_PAIR),
        in_specs=[pl.BlockSpec((1, 1, 1, TOPK_TILES, LIST_LANES), lambda bi, g, i: (bi, g, i // per_top, 0, 0),
                               memory_space=pltpu.SMEM),
                  tile(dh), per_bg(s, dh), per_bg(dh, s),
                  pl.BlockSpec((1, 1, 1, ns, TOPK_TILES, tq),
                               lambda bi, g, i: (bi, g, i // per_top, 0, 0, 0)),
                  tile(3),
                  pl.BlockSpec((1, 2, SUPER, n), lambda bi, g, i: (g, 0, 0, 0)),
                  pl.BlockSpec((1, 1, n), lambda bi, g, i: (g, 0, 0)),
                  tile(dh)],
        out_specs=tile(dh),
        out_shape=jax.ShapeDtypeStruct((b, kv, nqt, dh, n), BF16),
        compiler_params=_cparams(("parallel", "parallel", "arbitrary")),
        name="attn2",
    )(lst, qT, ks, vsT, sel_t, gatesT, t_s, srow, part)


def _alibi_slopes(n):
    return np.array([2.0 ** (-8.0 * (h + 1) / n) for h in range(n)], dtype=np.float32)


def _overlap_T(s):
    ncp = s // CMP_STRIDE
    nc = (s - CMP_LEN) // CMP_STRIDE + 1
    ns = s // SEL_BLOCK
    c_start = np.arange(ncp) * CMP_STRIDE
    s_start = np.arange(ns) * SEL_BLOCK
    ov = ((c_start[None, :] <= s_start[:, None] + SEL_BLOCK - 1) &
          (c_start[None, :] + CMP_LEN - 1 >= s_start[:, None]) & (np.arange(ncp)[None, :] < nc))
    return jnp.asarray(ov.astype(np.float32), dtype=BF16)


def _bias_tables(s):
    tq = ATT_TQ
    ncp = s // CMP_STRIDE
    slopes = jnp.asarray(_alibi_slopes(NSA_HEADS)).reshape(NSA_KV_HEADS, NSA_GROUP)
    srow = jnp.repeat(slopes, tq, axis=1)[:, None, :] * LOG2E
    t_rel = jnp.tile(jnp.arange(tq, dtype=jnp.int32), NSA_GROUP)[None, None, :]

    def table(rel_pos, lo, hi):
        dist = t_rel - rel_pos[None, :, None]
        bias = -srow * dist.astype(F32)
        return jnp.where((dist >= lo) & (dist < hi), bias, NEG)

    far = 1 << 30
    crel = (jnp.arange(2 * ncp, dtype=jnp.int32) - ncp) * CMP_STRIDE + (CMP_LEN - 1)
    t_c = table(crel, 0, far)
    t_w = table(jnp.arange(WINDOW + tq, dtype=jnp.int32) - WINDOW, 0, WINDOW)
    blk = jnp.arange(SUPER, dtype=jnp.int32)
    t_s = jnp.stack([table(blk, -far, far), table(blk, 0, far)], axis=1)
    return t_c, t_w, t_s, srow


def _nsa(qT, kvs, ng, cmp_pos, w_ck1, w_ck2, w_cv1, w_cv2):
    b, s, _ = kvs.shape
    kv, grp, dh = NSA_KV_HEADS, NSA_GROUP, HEAD_DIM
    tq = ATT_TQ
    nqt = s // tq
    ns = s // SEL_BLOCK
    kc, vc, ks, vs, kw, vw = [kvs[..., i * KV_WIDTH:(i + 1) * KV_WIDTH] for i in range(6)]
    ckf = _compress(kc, cmp_pos, w_ck1, w_ck2)
    cvf = _compress(vc, cmp_pos, w_cv1, w_cv2)
    heads = lambda v: v.reshape(b, -1, kv, dh)
    rows = lambda v: heads(v).transpose(0, 2, 1, 3)
    cols = lambda v: heads(v).transpose(0, 2, 3, 1)
    tiles = lambda v, c: v.reshape(b, nqt, tq, kv, grp, c).transpose(0, 3, 1, 5, 4, 2).reshape(
        b, kv, nqt, c, grp * tq)
    gatesT = tiles(ng, 3)
    t_c, t_w, t_s, srow = _bias_tables(s)

    part, imp = _attn1(qT, rows(ckf), cols(cvf), rows(kw), cols(vw), gatesT, t_c, t_w, _overlap_T(s))
    imp_t = imp.reshape(b, kv, nqt // TOPK_TILES, TOPK_TILES, ns, tq).transpose(0, 1, 2, 4, 3, 5)
    sel_t, lst = _topk(imp_t)
    return _attn2(lst, qT, rows(ks), cols(vs), sel_t, gatesT, t_s, srow, part)


HALO = 8
CONV_SLAB = 512


def _softplus(x):
    return jnp.maximum(x, 0.0) + jnp.log1p(jnp.exp(-jnp.abs(x)))


def _ssd_kernel(raw_ref, z_ref, dt_ref, dtT_ref, cw_ref, cb_ref, dtb_ref, dtbT_ref,
                al_ref, alT_ref, dsk_ref, nw_ref, o_ref, st_ref, tail_ref, ext_ref, xc_ref, y_ref):
    q = SSM_CHUNK

    @pl.when(pl.program_id(1) == 0)
    def _():
        st_ref[...] = jnp.zeros(st_ref.shape, F32)
        tail_ref[...] = jnp.zeros(tail_ref.shape, F32)

    ext_ref[0:HALO, :] = tail_ref[...]
    ext_ref[HALO:HALO + q, :] = raw_ref[0].astype(F32)
    tail_ref[...] = ext_ref[q:q + HALO, :]
    for c0 in range(0, CONV_DIM, CONV_SLAB):
        sl = slice(c0, c0 + CONV_SLAB)
        y = cb_ref[:, sl]
        for k in range(CONV_WIDTH):
            off = HALO - (CONV_WIDTH - 1) + k
            y = y + cw_ref[k:k + 1, sl] * ext_ref[off:off + q, sl]
        xc_ref[:, sl] = y * jax.nn.sigmoid(y)

    dt = _softplus(dt_ref[0][:, :SSM_HEADS] + dtb_ref[...])
    dtT = _softplus(dtT_ref[0] + dtbT_ref[...])
    ri = lax.broadcasted_iota(jnp.int32, (q, q), 0)
    ci = lax.broadcasted_iota(jnp.int32, (q, q), 1)
    tri = ri >= ci
    low = jnp.where(tri, 1.0, 0.0).astype(BF16)
    upp = jnp.where(ri <= ci, 1.0, 0.0).astype(BF16)
    cum = _dot01_l(low, dt * (-jnp.exp(al_ref[...])))
    cumT = _dot01_r(dtT * (-jnp.exp(alT_ref[...])), upp)
    ecum = jnp.exp(cum)
    lane_lo = lax.broadcasted_iota(jnp.int32, (1, LANES), 1) < SSM_HEAD_DIM
    b0 = SSM_INNER
    c0 = SSM_INNER + SSM_GROUPS * SSM_STATE

    for g in range(SSM_GROUPS):
        bgf = xc_ref[:, b0 + g * SSM_STATE:b0 + (g + 1) * SSM_STATE]
        cg = xc_ref[:, c0 + g * SSM_STATE:c0 + (g + 1) * SSM_STATE].astype(BF16)
        cb = lax.dot_general(cg, bgf.astype(BF16), (((1,), (1,)), ((), ())), preferred_element_type=F32)
        bTg = bgf.T
        for pr in range(SSM_HPG // 2):
            hp = g * (SSM_HPG // 2) + pr
            xp = xc_ref[:, hp * LANES:(hp + 1) * LANES].astype(BF16)
            ys, sts, ecs, els = [], [], [], []
            for k in range(2):
                h = 2 * hp + k
                crow = cumT[h:h + 1, :]
                seg = cum[:, h:h + 1] - crow
                w = cb * jnp.exp(jnp.where(tri, seg, NEG)) * dtT[h:h + 1, :]
                ys.append(jnp.dot(w.astype(BF16), xp, preferred_element_type=F32))
                clast = crow[:, q - 1:q]
                to_end = jnp.exp(clast - crow) * dtT[h:h + 1, :]
                sts.append(jnp.dot((bTg * to_end).astype(BF16), xp, preferred_element_type=F32))
                ecs.append(ecum[:, h:h + 1])
                els.append(jnp.exp(clast))
            st = st_ref[hp]
            y_in = jnp.dot(cg, st.astype(BF16), preferred_element_type=F32)
            y_ref[:, hp * LANES:(hp + 1) * LANES] = (
                jnp.where(lane_lo, ys[0], ys[1]) + y_in * jnp.where(lane_lo, ecs[0], ecs[1]))
            st_ref[hp] = st * jnp.where(lane_lo, els[0], els[1]) + jnp.where(lane_lo, sts[0], sts[1])

    gw = SSM_INNER // SSM_GROUPS
    for g in range(SSM_GROUPS):
        sl = slice(g * gw, (g + 1) * gw)
        z = z_ref[0][:, sl].astype(F32)
        y = (y_ref[:, sl] + dsk_ref[:, sl] * xc_ref[:, sl]) * (z * jax.nn.sigmoid(z))
        y = y * lax.rsqrt(jnp.mean(y * y, axis=-1, keepdims=True) + EPS)
        o_ref[0, :, sl] = (y * nw_ref[:, sl]).astype(o_ref.dtype)


def _ssd(xbc, z, dts, conv_w, conv_b, dt_bias, a_log, d_skip, ssm_norm):
    b, s, _ = xbc.shape
    q = SSM_CHUNK
    hh = SSM_HEADS
    dtT = dts[..., :hh].transpose(0, 2, 1)
    row = lambda v: v.reshape(1, -1).astype(F32)
    col = lambda v: v.reshape(-1, 1).astype(F32)
    const = lambda shape: pl.BlockSpec(shape, lambda bi, c: (0, 0))
    return pl.pallas_call(
        _ssd_kernel,
        grid=(b, s // q),
        in_specs=[pl.BlockSpec((1, q, CONV_DIM), lambda bi, c: (bi, c, 0)),
                  pl.BlockSpec((1, q, SSM_INNER), lambda bi, c: (bi, c, 0)),
                  pl.BlockSpec((1, q, LANES), lambda bi, c: (bi, c, 0)),
                  pl.BlockSpec((1, hh, q), lambda bi, c: (bi, 0, c)),
                  const((CONV_WIDTH, CONV_DIM)), const((1, CONV_DIM)),
                  const((1, hh)), const((hh, 1)), const((1, hh)), const((hh, 1)),
                  const((1, SSM_INNER)), const((1, SSM_INNER))],
        out_specs=pl.BlockSpec((1, q, SSM_INNER), lambda bi, c: (bi, c, 0)),
        out_shape=jax.ShapeDtypeStruct((b, s, SSM_INNER), BF16),
        scratch_shapes=[pltpu.VMEM((hh // 2, SSM_STATE, LANES), F32),
                        pltpu.VMEM((HALO, CONV_DIM), F32),
                        pltpu.VMEM((q + HALO, CONV_DIM), F32),
                        pltpu.VMEM((q, CONV_DIM), F32),
                        pltpu.VMEM((q, SSM_INNER), F32)],
        compiler_params=_cparams(("parallel", "arbitrary")),
        name="ssd",
    )(xbc, z, dts, dtT, conv_w, row(conv_b), row(dt_bias), col(dt_bias), row(a_log), col(a_log),
      row(jnp.repeat(d_skip, SSM_HEAD_DIM)), row(ssm_norm))


MERGE_TM = 512


def _merge_kernel(h_ref, y_ref, ym_ref, mg_ref, waT_ref, ws_ref, wo_ref, post_ref, o_ref):
    d = h_ref.shape[1]
    tq = y_ref.shape[4] // NSA_GROUP
    yT = jnp.concatenate(
        [jnp.concatenate([y_ref[0, kv, t, :, h * tq:(h + 1) * tq]
                          for kv in range(NSA_KV_HEADS) for h in range(NSA_GROUP)], axis=0)
         for t in range(y_ref.shape[2])], axis=1)
    a = jnp.dot(waT_ref[...], yT, preferred_element_type=F32).T
    m = jnp.dot(ym_ref[...], ws_ref[...], preferred_element_type=F32)
    gts = jax.nn.sigmoid(mg_ref[...].astype(F32))
    merged = gts[:, :d] * a + gts[:, d:] * m
    out = jnp.dot(merged.astype(BF16), wo_ref[...], preferred_element_type=F32)
    o_ref[...] = h_ref[...] + _rms(out, post_ref[...])


def _merge(h2d, ya_tiles, ym, mg, wa, ws, wo, post):
    t, d = h2d.shape
    b, kv, nqt, dh, n = ya_tiles.shape
    tq = n // NSA_GROUP
    s = nqt * tq
    tm = min(MERGE_TM, s)
    per_seq = s // tm
    const = lambda shape: pl.BlockSpec(shape, lambda i: (0, 0), pipeline_mode=pl.Buffered(1))
    rows = lambda c: pl.BlockSpec((tm, c), lambda i: (i, 0))
    return pl.pallas_call(
        _merge_kernel,
        grid=(t // tm,),
        in_specs=[rows(d),
                  pl.BlockSpec((1, kv, tm // tq, dh, n), lambda i: (i // per_seq, 0, i % per_seq, 0, 0)),
                  rows(SSM_INNER), rows(2 * d),
                  const((d, NSA_WIDTH)), const((SSM_INNER, d)), const((d, d)), const((1, d))],
        out_specs=rows(d),
        out_shape=jax.ShapeDtypeStruct((t, d), F32),
        compiler_params=_cparams(("parallel",)),
        name="merge",
    )(h2d, ya_tiles, ym, mg, wa.T.astype(BF16), ws.astype(BF16), wo.astype(BF16), post.reshape(1, d))


def _mixer(h2d, b, s, mix_pre, w_in, cmp_pos, w_ck1, w_ck2, w_cv1, w_cv2, conv_w, conv_b, dt_bias,
           a_log, d_skip, ssm_norm, w_attn_branch, w_ssm_branch, w_out, mix_post):
    o = 0
    cols = {}
    for name, width in (("q", NSA_WIDTH), ("kv", 6 * KV_WIDTH), ("ng", NSA_HEADS * 3), ("z", SSM_INNER),
                        ("xbc", CONV_DIM), ("dt", SSM_HEADS), ("mg", 2 * D_MODEL)):
        cols[name] = w_in[:, o:o + width]
        o += width
    pad = LANES - SSM_HEADS - NSA_HEADS * 3
    w_small = jnp.concatenate([cols["dt"], cols["ng"], jnp.zeros((D_MODEL, pad), F32)], axis=1)

    u = _rmsnorm(h2d, mix_pre)
    qT = _proj_q(u, cols["q"], b, s, HEAD_DIM ** -0.5 * LOG2E, ATT_TQ)
    kvs = _proj(u, cols["kv"], BF16).reshape(b, s, 6 * KV_WIDTH)
    z = _proj(u, cols["z"], BF16).reshape(b, s, SSM_INNER)
    xbc = _proj(u, cols["xbc"], BF16).reshape(b, s, CONV_DIM)
    mg = _proj(u, cols["mg"], BF16)
    small = _proj(u, w_small, F32).reshape(b, s, LANES)
    ng = small[..., SSM_HEADS:SSM_HEADS + NSA_HEADS * 3]

    y_a = _nsa(qT, kvs, ng, cmp_pos, w_ck1, w_ck2, w_cv1, w_cv2)
    y_m = _ssd(xbc, z, small, conv_w, conv_b, dt_bias, a_log, d_skip, ssm_norm)
    return _merge(h2d, y_a, y_m.reshape(b * s, SSM_INNER), mg,
                  w_attn_branch, w_ssm_branch, w_out, mix_post)


def kernel(x, ffn1_pre, ffn1_gate, ffn1_up, ffn1_down, ffn1_post, mix_pre, w_in, cmp_pos, w_ck1, w_ck2,
           w_cv1, w_cv2, conv_w, conv_b, dt_bias, a_log, d_skip, ssm_norm, w_attn_branch, w_ssm_branch,
           w_out, mix_post, ffn2_pre, ffn2_gate, ffn2_up, ffn2_down, ffn2_post):
    b, s, d = x.shape
    h = x.reshape(b * s, d)
    for l in range(ffn1_pre.shape[0]):
        h = _ffn(h, ffn1_pre[l], ffn1_gate[l], ffn1_up[l], ffn1_down[l], ffn1_post[l])
        h = _mixer(h, b, s, mix_pre[l], w_in[l], cmp_pos[l], w_ck1[l], w_ck2[l], w_cv1[l], w_cv2[l],
                   conv_w[l], conv_b[l], dt_bias[l], a_log[l], d_skip[l], ssm_norm[l],
                   w_attn_branch[l], w_ssm_branch[l], w_out[l], mix_post[l])
        h = _ffn(h, ffn2_pre[l], ffn2_gate[l], ffn2_up[l], ffn2_down[l], ffn2_post[l])
    return h.reshape(b, s, d)
```

```python
import functools

import numpy as np
import jax
import jax.numpy as jnp
from jax import lax
from jax.experimental import pallas as pl
from jax.experimental.pallas import tpu as pltpu

F32 = jnp.float32
BF16 = jnp.bfloat16

D_MODEL = 1024
NSA_HEADS = 16
NSA_KV_HEADS = 4
NSA_GROUP = NSA_HEADS // NSA_KV_HEADS
HEAD_DIM = 64
CMP_LEN = 32
CMP_STRIDE = 16
CMP_HIDDEN = 2 * HEAD_DIM
SEL_BLOCK = 64
SEL_TOPK = 16
WINDOW = 512
NSA_WIDTH = NSA_HEADS * HEAD_DIM
KV_WIDTH = NSA_KV_HEADS * HEAD_DIM
BIG = 1e4
NEG = -1e30

SSM_INNER = 2 * D_MODEL
SSM_HEAD_DIM = 64
SSM_HEADS = SSM_INNER // SSM_HEAD_DIM
SSM_GROUPS = 4
SSM_HPG = SSM_HEADS // SSM_GROUPS
SSM_STATE = 128
CONV_WIDTH = 4
SSM_CHUNK = 256
CONV_DIM = SSM_INNER + 2 * SSM_GROUPS * SSM_STATE
FFN_HIDDEN = 2816
EPS = 1e-6
LOG2E = 1.4426950408889634

LANES = 128
VMEM_LIMIT = 56 * 1024 * 1024


def _cparams(sem):
    return pltpu.CompilerParams(dimension_semantics=sem, vmem_limit_bytes=VMEM_LIMIT)


def _rms(x, g):
    return x * lax.rsqrt(jnp.mean(x * x, axis=-1, keepdims=True) + EPS) * g


def _split3(x):
    hi = x.astype(BF16)
    r1 = x - hi.astype(F32)
    mid = r1.astype(BF16)
    lo = (r1 - mid.astype(F32)).astype(BF16)
    return hi, mid, lo


def _dot01_l(a01, x):
    hi, mid, lo = _split3(x)
    d = lambda v: jnp.dot(a01, v, preferred_element_type=F32)
    return d(hi) + (d(mid) + d(lo))


def _dot01_r(x, a01):
    hi, mid, lo = _split3(x)
    d = lambda v: jnp.dot(v, a01, preferred_element_type=F32)
    return d(hi) + (d(mid) + d(lo))


def _interleave(gens):
    live = list(gens)
    while live:
        live = [g for g in live if next(g, StopIteration) is not StopIteration]


FFN_TM = 256
FFN_HC = 256
FFN_TILES = 4


def _ffn_tile(t, x_ref, pre_ref, wg_ref, wu_ref, wd_ref, post_ref, o_ref):
    rows = pl.ds(t * FFN_TM, FFN_TM)
    x = x_ref[rows, :]
    ub = _rms(x, pre_ref[...]).astype(BF16)
    yield
    acc = jnp.zeros(x.shape, F32)
    for c in range(FFN_HIDDEN // FFN_HC):
        sl = slice(c * FFN_HC, (c + 1) * FFN_HC)
        g = jnp.dot(ub, wg_ref[:, sl], preferred_element_type=F32)
        u = jnp.dot(ub, wu_ref[:, sl], preferred_element_type=F32)
        a = (g * jax.nn.sigmoid(g)) * u
        acc = acc + jnp.dot(a.astype(BF16), wd_ref[sl, :], preferred_element_type=F32)
        yield
    o_ref[rows, :] = x + 0.5 * _rms(acc, post_ref[...])
    yield


def _ffn_kernel(*refs):
    _interleave([_ffn_tile(t, *refs) for t in range(FFN_TILES)])


def _ffn(h2d, pre, wg, wu, wd, post):
    t, d = h2d.shape
    tm = min(FFN_TM * FFN_TILES, t)
    const = lambda shape: pl.BlockSpec(shape, lambda i: (0, 0), pipeline_mode=pl.Buffered(1))
    return pl.pallas_call(
        _ffn_kernel,
        grid=(t // tm,),
        in_specs=[pl.BlockSpec((tm, d), lambda i: (i, 0)),
                  const((1, d)), const((d, FFN_HIDDEN)), const((d, FFN_HIDDEN)),
                  const((FFN_HIDDEN, d)), const((1, d))],
        out_specs=pl.BlockSpec((tm, d), lambda i: (i, 0)),
        out_shape=jax.ShapeDtypeStruct((t, d), F32),
        compiler_params=_cparams(("parallel",)),
        name="ffn",
    )(h2d, pre.reshape(1, d), wg.astype(BF16), wu.astype(BF16), wd.astype(BF16), post.reshape(1, d))


PROJ_TM = 1024
PROJ_TN = 1024


def _rmsnorm_kernel(x_ref, g_ref, o_ref):
    o_ref[...] = _rms(x_ref[...], g_ref[...]).astype(o_ref.dtype)


def _rmsnorm(h2d, g):
    t, d = h2d.shape
    tm = min(PROJ_TM, t)
    return pl.pallas_call(
        _rmsnorm_kernel,
        grid=(t // tm,),
        in_specs=[pl.BlockSpec((tm, d), lambda i: (i, 0)), pl.BlockSpec((1, d), lambda i: (0, 0))],
        out_specs=pl.BlockSpec((tm, d), lambda i: (i, 0)),
        out_shape=jax.ShapeDtypeStruct((t, d), BF16),
        compiler_params=_cparams(("parallel",)),
        name="rmsnorm",
    )(h2d, g.reshape(1, d))


def _proj_kernel(u_ref, w_ref, o_ref, *, scale):
    acc = jnp.dot(u_ref[...], w_ref[...], preferred_element_type=F32)
    if scale != 1.0:
        acc = acc * scale
    o_ref[...] = acc.astype(o_ref.dtype)


def _proj(u, w, out_dtype, scale=1.0):
    t, d = u.shape
    n = w.shape[1]
    tm = min(PROJ_TM, t)
    tn = next(c for c in range(min(PROJ_TN, n), 0, -LANES) if n % c == 0)
    return pl.pallas_call(
        functools.partial(_proj_kernel, scale=scale),
        grid=(t // tm, n // tn),
        in_specs=[pl.BlockSpec((tm, d), lambda i, j: (i, 0)),
                  pl.BlockSpec((d, tn), lambda i, j: (0, j))],
        out_specs=pl.BlockSpec((tm, tn), lambda i, j: (i, j)),
        out_shape=jax.ShapeDtypeStruct((t, n), out_dtype),
        compiler_params=_cparams(("parallel", "parallel")),
        name="proj",
    )(u, w.astype(BF16))


def _proj_q_kernel(u_ref, wT_ref, o_ref, *, scale, tq):
    r = lax.dot_general(wT_ref[...], u_ref[...], (((1,), (1,)), ((), ())), preferred_element_type=F32)
    r = (r * scale).astype(o_ref.dtype)
    for g in range(NSA_KV_HEADS):
        for t in range(u_ref.shape[0] // tq):
            for h in range(NSA_GROUP):
                r0 = (g * NSA_GROUP + h) * HEAD_DIM
                o_ref[0, g, t, :, h * tq:(h + 1) * tq] = r[r0:r0 + HEAD_DIM, t * tq:(t + 1) * tq]


def _proj_q(u, w, b, s, scale, tq):
    t, d = u.shape
    tm = min(PROJ_TM, s)
    per_seq = s // tm
    return pl.pallas_call(
        functools.partial(_proj_q_kernel, scale=scale, tq=tq),
        grid=(t // tm,),
        in_specs=[pl.BlockSpec((tm, d), lambda i: (i, 0)),
                  pl.BlockSpec((NSA_WIDTH, d), lambda i: (0, 0))],
        out_specs=pl.BlockSpec((1, NSA_KV_HEADS, tm // tq, HEAD_DIM, NSA_GROUP * tq),
                               lambda i: (i // per_seq, 0, i % per_seq, 0, 0)),
        out_shape=jax.ShapeDtypeStruct((b, NSA_KV_HEADS, s // tq, HEAD_DIM, NSA_GROUP * tq), BF16),
        compiler_params=_cparams(("parallel",)),
        name="proj_q",
    )(u, w.T.astype(BF16))


def _compress_kernel(r_ref, pt_ref, pb_ref, w1t_ref, w1b_ref, w2_ref, o_ref):
    r = r_ref[0].astype(F32)
    top = jnp.dot((r + pt_ref[...]).astype(BF16), w1t_ref[...], preferred_element_type=F32)
    bot = jnp.dot((r + pb_ref[...]).astype(BF16), w1b_ref[...], preferred_element_type=F32)
    ncp = r.shape[0]
    h = top + pltpu.roll(bot, ncp - 1, 0)
    a = (h * jax.nn.sigmoid(h)).astype(BF16)
    o_ref[0] = jnp.dot(a, w2_ref[...], preferred_element_type=F32).astype(o_ref.dtype)


def _compress(k, pos, w1, w2):
    b, s, _ = k.shape
    ncp = s // CMP_STRIDE
    half = CMP_LEN // 2
    rw = half * KV_WIDTH
    eye = jnp.eye(NSA_KV_HEADS, dtype=F32)
    w1r = w1.reshape(CMP_LEN, HEAD_DIM, CMP_HIDDEN)

    def big(wpart):
        return jnp.einsum('ldj,gk->lgdkj', wpart, eye).reshape(rw, NSA_KV_HEADS * CMP_HIDDEN).astype(BF16)

    def posrow(p):
        return jnp.broadcast_to(p[:, None, :], (half, NSA_KV_HEADS, HEAD_DIM)).reshape(1, rw)

    w2big = jnp.einsum('jd,gk->gjkd', w2, eye).reshape(NSA_KV_HEADS * CMP_HIDDEN, KV_WIDTH).astype(BF16)
    nh = NSA_KV_HEADS * CMP_HIDDEN
    const = lambda shape: pl.BlockSpec(shape, lambda i: (0, 0))
    return pl.pallas_call(
        _compress_kernel,
        grid=(b,),
        in_specs=[pl.BlockSpec((1, ncp, rw), lambda i: (i, 0, 0)),
                  const((1, rw)), const((1, rw)), const((rw, nh)), const((rw, nh)),
                  const((nh, KV_WIDTH))],
        out_specs=pl.BlockSpec((1, ncp, KV_WIDTH), lambda i: (i, 0, 0)),
        out_shape=jax.ShapeDtypeStruct((b, ncp, KV_WIDTH), BF16),
        compiler_params=_cparams(("parallel",)),
        name="compress",
    )(k.reshape(b, ncp, rw), posrow(pos[:half]), posrow(pos[half:]), big(w1r[:half]), big(w1r[half:]), w2big)


ATT_TQ = 128
SUPER = 2 * SEL_BLOCK
SEL_UNROLL = 5
LIST_LANES = LANES
TOPK_TILES = 8
ATT_PAIR = 8


def _softmax_steps(states, blocks_list):
    n = states[0][0].shape[1]
    ms, ls, accs = [], [], []
    for (m, l8, acc), blocks in zip(states, blocks_list):
        part = None
        for s, rows, _ in blocks:
            hr = s.shape[0] // len(rows)
            for a, r in enumerate(rows):
                c = jnp.max(s[a * hr:(a + 1) * hr].reshape(hr // 8, 8, n), axis=0) + r
                part = c if part is None else jnp.maximum(part, c)
        m_new = jnp.maximum(m, jnp.max(part, axis=0, keepdims=True))
        alpha = jnp.exp2(m - m_new)
        ms.append(m_new)
        ls.append(l8 * alpha)
        accs.append(acc * alpha)
    for u in range(max(len(blocks) for blocks in blocks_list)):
        for k, blocks in enumerate(blocks_list):
            if u >= len(blocks):
                continue
            s, rows, vT = blocks[u]
            hr = s.shape[0] // len(rows)
            ps = [jnp.exp2(s[a * hr:(a + 1) * hr] - (ms[k] - r)) for a, r in enumerate(rows)]
            p = ps[0] if len(ps) == 1 else jnp.concatenate(ps, axis=0)
            ls[k] = ls[k] + jnp.sum(p.reshape(p.shape[0] // 8, 8, n), axis=0)
            accs[k] = accs[k] + jnp.dot(vT, p.astype(BF16), preferred_element_type=F32)
    return [(ms[k], ls[k], accs[k]) for k in range(len(states))]


def _softmax_step(state, blocks):
    return _softmax_steps([state], [blocks])[0]


def _softmax_init(n):
    return (jnp.full((1, n), NEG, F32), jnp.zeros((8, n), F32), jnp.zeros((HEAD_DIM, n), F32))


def _attn1_tile(t, i, qT_ref, ck_ref, cvT_ref, kw_ref, vwT_ref, gates_ref, tc_ref, tw_ref, ovT_ref,
                part_ref, imp_ref, tq, ncp, rows):
    q0 = i * tq
    n = NSA_GROUP * tq
    qT = qT_ref[0, 0, t]

    n_kt = WINDOW // tq + 1
    blocks = []
    for kt in range(n_kt):
        k0 = q0 - WINDOW + kt * tq
        pen = jnp.where(k0 < 0, NEG, 0.0)
        k0 = pl.multiple_of(jnp.maximum(k0, 0), tq)
        sw = jnp.dot(kw_ref[0, 0, pl.ds(k0, tq), :], qT, preferred_element_type=F32) + \
            tw_ref[0, kt * tq:(kt + 1) * tq, :]
        blocks.append((sw, [pen], vwT_ref[0, 0, :, pl.ds(k0, tq)]))
    yield

    start = pl.multiple_of(ncp - i * (tq // CMP_STRIDE), 8)
    s = jnp.dot(ck_ref[0, 0, 0:rows, :], qT, preferred_element_type=F32) + tc_ref[0, pl.ds(start, rows), :]
    yield
    e = jnp.exp2(s - jnp.max(s, axis=0, keepdims=True))
    yield
    t_rel = jnp.bitwise_and(lax.broadcasted_iota(jnp.int32, (1, n), 1), tq - 1)
    inv = jnp.where(q0 + t_rel >= CMP_LEN - 1, 1.0 / jnp.sum(e, axis=0, keepdims=True), 0.0)
    p = e * inv
    o_c = jnp.dot(cvT_ref[0, 0, :, 0:rows], p.astype(BF16), preferred_element_type=F32)
    yield
    psum = p[:, 0:tq]
    for h in range(1, NSA_GROUP):
        psum = psum + p[:, h * tq:(h + 1) * tq]
    imp_ref[0, 0, t] = _dot01_l(ovT_ref[:, 0:rows], psum)
    yield

    _, l_w, acc_w = _softmax_step(_softmax_init(n), blocks)
    l_w = jnp.sum(l_w, axis=0, keepdims=True)
    yield

    gsig = jax.nn.sigmoid(gates_ref[0, 0, t])
    part_ref[0, 0, t] = gsig[0:1, :] * o_c + gsig[2:3, :] * (acc_w * (1.0 / l_w))
    yield


def _attn1_kernel(*refs, tq, ncp):
    step = pl.program_id(2)
    i0 = step * ATT_PAIR
    half = pl.num_programs(2) // 2

    def run(rows):
        _interleave([_attn1_tile(t, i0 + t, *refs, tq, ncp, rows) for t in range(ATT_PAIR)])

    @pl.when(step < half)
    def _():
        run(ncp // 2)

    @pl.when(step >= half)
    def _():
        run(ncp)


def _attn1(qT, ck, cvT, kw, vwT, gatesT, t_c, t_w, ovT):
    b, kv, nqt, dh, n = qT.shape
    tq = n // NSA_GROUP
    s = nqt * tq
    ncp = ck.shape[2]
    ns = s // SEL_BLOCK
    kern = functools.partial(_attn1_kernel, tq=tq, ncp=ncp)
    tile = lambda r, c: pl.BlockSpec((1, 1, ATT_PAIR, r, c), lambda bi, g, i: (bi, g, i, 0, 0))
    per_bg = lambda r, c: pl.BlockSpec((1, 1, r, c), lambda bi, g, i: (bi, g, 0, 0))
    per_g = lambda r: pl.BlockSpec((1, r, n), lambda bi, g, i: (g, 0, 0))
    return pl.pallas_call(
        kern,
        grid=(b, kv, nqt // ATT_PAIR),
        in_specs=[tile(dh, n), per_bg(ncp, dh), per_bg(dh, ncp), per_bg(s, dh), per_bg(dh, s), tile(3, n),
                  per_g(2 * ncp), per_g(WINDOW + tq),
                  pl.BlockSpec((ns, ncp), lambda bi, g, i: (0, 0))],
        out_specs=[tile(dh, n), tile(ns, tq)],
        out_shape=[jax.ShapeDtypeStruct((b, kv, nqt, dh, n), F32),
                   jax.ShapeDtypeStruct((b, kv, nqt, ns, tq), F32)],
        compiler_params=_cparams(("parallel", "parallel", "arbitrary")),
        name="attn1",
    )(qT, ck, cvT, kw, vwT, gatesT, t_c, t_w, ovT)


def _topk_kernel(imp_ref, pair_ref, low_ref, pick_ref, sel_ref, lst_ref, *, ns, n_sel, tq):
    c = pl.program_id(2)
    shape = (ns, TOPK_TILES, tq)
    j = lax.broadcasted_iota(jnp.int32, shape, 0)
    jf = j.astype(F32)
    t = (c * TOPK_TILES + lax.broadcasted_iota(jnp.int32, shape, 1)) * tq + \
        lax.broadcasted_iota(jnp.int32, shape, 2)
    cur = jnp.right_shift(t, SEL_BLOCK.bit_length() - 1)
    valid = j <= cur
    forced = (j == 0) | (j == cur) | (j == cur - 1)
    work = jnp.where(forced, -jnp.inf, jnp.where(valid, imp_ref[0, 0, 0], -BIG))
    for _ in range(n_sel - 3):
        v, k = work, jf
        while v.shape[0] > 1:
            h = v.shape[0] // 2
            hi = v[h:] > v[:h]
            v = jnp.where(hi, v[h:], v[:h])
            k = jnp.where(hi, k[h:], k[:h])
        work = jnp.where(jf == k, -jnp.inf, work)
    selm = jnp.where(valid & (work == -jnp.inf), 1.0, 0.0)
    sel_ref[0, 0, 0] = selm

    nsb = ns // 2
    any_q = jnp.broadcast_to(jnp.max(selm, axis=2, keepdims=True), shape[:2] + (LIST_LANES,))
    any_q = any_q.reshape(ns * TOPK_TILES, LIST_LANES).astype(BF16)
    rows = lax.broadcasted_iota(jnp.int32, (nsb * TOPK_TILES, 1), 0)
    sb = jnp.right_shift(rows, 3)
    tile = c * TOPK_TILES + jnp.bitwise_and(rows, TOPK_TILES - 1)
    flag = jnp.dot(pair_ref[...], any_q, preferred_element_type=F32)
    flag = jnp.where((flag > 0.0) & (sb < tile), 1.0, 0.0)
    rank = jnp.dot(low_ref[...], flag.astype(BF16), preferred_element_type=F32)
    k_row = lax.broadcasted_iota(jnp.int32, (1, LIST_LANES), 1)
    hit = jnp.where((rank == (k_row + 1).astype(F32)) & (flag > 0.0), 1.0, 0.0).astype(BF16)
    lst = jnp.dot(pick_ref[...], hit, preferred_element_type=F32)
    total = rank[(nsb - 1) * TOPK_TILES:nsb * TOPK_TILES, :]
    lst_ref[0, 0, 0] = jnp.where(k_row == LIST_LANES - 1, total, lst).astype(jnp.int32)


def _topk(imp_t):
    b, kv, nc, ns, tiles, tq = imp_t.shape
    nsb = ns // 2
    r = np.arange(nsb * tiles)
    q = np.arange(ns * tiles)
    same = (r[:, None] % tiles) == (q[None, :] % tiles)
    pair = same & ((q[None, :] // tiles) // 2 == (r[:, None] // tiles))
    low = ((r[:, None] % tiles) == (r[None, :] % tiles)) & ((r[None, :] // tiles) <= (r[:, None] // tiles))
    pick = (np.arange(tiles)[:, None] == (r[None, :] % tiles)) * (r[None, :] // tiles)
    const = lambda a: jnp.asarray(a.astype(np.float32), dtype=BF16)
    whole = lambda a: pl.BlockSpec(a.shape, lambda bi, g, c: (0, 0))
    blk = pl.BlockSpec((1, 1, 1, ns, tiles, tq), lambda bi, g, c: (bi, g, c, 0, 0, 0))
    return pl.pallas_call(
        functools.partial(_topk_kernel, ns=ns, n_sel=min(SEL_TOPK, ns), tq=tq),
        grid=(b, kv, nc),
        in_specs=[blk, whole(pair), whole(low), whole(pick)],
        out_specs=[blk, pl.BlockSpec((1, 1, 1, tiles, LIST_LANES), lambda bi, g, c: (bi, g, c, 0, 0))],
        out_shape=[jax.ShapeDtypeStruct(imp_t.shape, F32),
                   jax.ShapeDtypeStruct((b, kv, nc, tiles, LIST_LANES), jnp.int32)],
        compiler_params=_cparams(("parallel", "parallel", "arbitrary")),
        name="topk",
    )(imp_t, const(pair), const(low), const(pick))


def _attn2_kernel(lst_ref, qT_ref, ks_ref, vsT_ref, sel_ref, gates_ref, ts_ref, srow_ref, part_ref,
                  o_ref, *, tq):
    n = NSA_GROUP * tq
    i0 = pl.program_id(2) * ATT_PAIR
    srow = srow_ref[0]
    tiles = [(t, i0 + t, jnp.bitwise_and(i0 + t, TOPK_TILES - 1)) for t in range(ATT_PAIR)]
    cnts = [lst_ref[0, 0, 0, a, LIST_LANES - 1] for _, _, a in tiles]

    def sel_rows(a, sb, base):
        out = []
        for half in range(2):
            row = jnp.where(sel_ref[0, 0, 0, 2 * sb + half, pl.ds(a, 1), :] > 0.0, 0.0, NEG)
            out.append(jnp.concatenate([row] * NSA_GROUP, axis=1) + base)
        return out

    def scores(t, sb, table):
        k0 = pl.multiple_of(sb * SUPER, SUPER)
        s = jnp.dot(ks_ref[0, 0, pl.ds(k0, SUPER), :], qT_ref[0, 0, t], preferred_element_type=F32) + table
        return s, vsT_ref[0, 0, :, pl.ds(k0, SUPER)]

    def listed(tile, k):
        t, i, a = tile
        sb = lst_ref[0, 0, 0, a, k]
        pen = jnp.where(k < cnts[t], 0.0, NEG)
        base = srow * (sb * SUPER - i * tq).astype(F32) + pen
        s, vT = scores(t, sb, ts_ref[0, 0])
        return s, sel_rows(a, sb, base), vT

    def diagonal(tile):
        t, i, a = tile
        s, vT = scores(t, i, ts_ref[0, 1])
        return s, sel_rows(a, i, 0.0), vT

    def group(first_k, with_diag):
        blocks_list = [[diagonal(tile)] if with_diag else [] for tile in tiles]
        for u in range(SEL_UNROLL - with_diag):
            for tile in tiles:
                blocks_list[tile[0]].append(listed(tile, first_k + u))
        return blocks_list

    states = _softmax_steps([_softmax_init(n)] * ATT_PAIR, group(0, 1))

    def body(it, sts):
        return tuple(_softmax_steps(list(sts), group(SEL_UNROLL - 1 + it * SEL_UNROLL, 0)))

    rest = functools.reduce(jnp.maximum, cnts) - (SEL_UNROLL - 1)
    n_it = (jnp.maximum(rest, 0) + SEL_UNROLL - 1) // SEL_UNROLL
    states = lax.fori_loop(0, n_it, body, tuple(states))
    for t, _, _ in tiles:
        _, l_s, acc_s = states[t]
        l_s = jnp.sum(l_s, axis=0, keepdims=True)
        gsig = jax.nn.sigmoid(gates_ref[0, 0, t])
        o_ref[0, 0, t] = (part_ref[0, 0, t] + gsig[1:2, :] * (acc_s * (1.0 / l_s))).astype(o_ref.dtype)


def _attn2(lst, qT, ks, vsT, sel_t, gatesT, t_s, srow, part):
    b, kv, nqt, dh, n = qT.shape
    tq = n // NSA_GROUP
    s = nqt * tq
    ns = s // SEL_BLOCK
    per_top = TOPK_TILES // ATT_PAIR
    tile = lambda r: pl.BlockSpec((1, 1, ATT_PAIR, r, n), lambda bi, g, i: (bi, g, i, 0, 0))
    per_bg = lambda r, c: pl.BlockSpec((1, 1, r, c), lambda bi, g, i: (bi, g, 0, 0))
    return pl.pallas_call(
        functools.partial(_attn2_kernel, tq=tq),
        grid=(b, kv, nqt // ATT_PAIR),
        in_specs=[pl.BlockSpec((1, 1, 1, TOPK_TILES, LIST_LANES), lambda bi, g, i: (bi, g, i // per_top, 0, 0),
                               memory_space=pltpu.SMEM),
                  tile(dh), per_bg(s, dh), per_bg(dh, s),
                  pl.BlockSpec((1, 1, 1, ns, TOPK_TILES, tq),
                               lambda bi, g, i: (bi, g, i // per_top, 0, 0, 0)),
                  tile(3),
                  pl.BlockSpec((1, 2, SUPER, n), lambda bi, g, i: (g, 0, 0, 0)),
                  pl.BlockSpec((1, 1, n), lambda bi, g, i: (g, 0, 0)),
                  tile(dh)],
        out_specs=tile(dh),
        out_shape=jax.ShapeDtypeStruct((b, kv, nqt, dh, n), BF16),
        compiler_params=_cparams(("parallel", "parallel", "arbitrary")),
        name="attn2",
    )(lst, qT, ks, vsT, sel_t, gatesT, t_s, srow, part)


def _alibi_slopes(n):
    return np.array([2.0 ** (-8.0 * (h + 1) / n) for h in range(n)], dtype=np.float32)


def _overlap_T(s):
    ncp = s // CMP_STRIDE
    nc = (s - CMP_LEN) // CMP_STRIDE + 1
    ns = s // SEL_BLOCK
    c_start = np.arange(ncp) * CMP_STRIDE
    s_start = np.arange(ns) * SEL_BLOCK
    ov = ((c_start[None, :] <= s_start[:, None] + SEL_BLOCK - 1) &
          (c_start[None, :] + CMP_LEN - 1 >= s_start[:, None]) & (np.arange(ncp)[None, :] < nc))
    return jnp.asarray(ov.astype(np.float32), dtype=BF16)


def _bias_tables(s):
    tq = ATT_TQ
    ncp = s // CMP_STRIDE
    slopes = jnp.asarray(_alibi_slopes(NSA_HEADS)).reshape(NSA_KV_HEADS, NSA_GROUP)
    srow = jnp.repeat(slopes, tq, axis=1)[:, None, :] * LOG2E
    t_rel = jnp.tile(jnp.arange(tq, dtype=jnp.int32), NSA_GROUP)[None, None, :]

    def table(rel_pos, lo, hi):
        dist = t_rel - rel_pos[None, :, None]
        bias = -srow * dist.astype(F32)
        return jnp.where((dist >= lo) & (dist < hi), bias, NEG)

    far = 1 << 30
    crel = (jnp.arange(2 * ncp, dtype=jnp.int32) - ncp) * CMP_STRIDE + (CMP_LEN - 1)
    t_c = table(crel, 0, far)
    t_w = table(jnp.arange(WINDOW + tq, dtype=jnp.int32) - WINDOW, 0, WINDOW)
    blk = jnp.arange(SUPER, dtype=jnp.int32)
    t_s = jnp.stack([table(blk, -far, far), table(blk, 0, far)], axis=1)
    return t_c, t_w, t_s, srow


def _nsa(qT, kvs, ng, cmp_pos, w_ck1, w_ck2, w_cv1, w_cv2):
    b, s, _ = kvs.shape
    kv, grp, dh = NSA_KV_HEADS, NSA_GROUP, HEAD_DIM
    tq = ATT_TQ
    nqt = s // tq
    ns = s // SEL_BLOCK
    kc, vc, ks, vs, kw, vw = [kvs[..., i * KV_WIDTH:(i + 1) * KV_WIDTH] for i in range(6)]
    ckf = _compress(kc, cmp_pos, w_ck1, w_ck2)
    cvf = _compress(vc, cmp_pos, w_cv1, w_cv2)
    heads = lambda v: v.reshape(b, -1, kv, dh)
    rows = lambda v: heads(v).transpose(0, 2, 1, 3)
    cols = lambda v: heads(v).transpose(0, 2, 3, 1)
    tiles = lambda v, c: v.reshape(b, nqt, tq, kv, grp, c).transpose(0, 3, 1, 5, 4, 2).reshape(
        b, kv, nqt, c, grp * tq)
    gatesT = tiles(ng, 3)
    t_c, t_w, t_s, srow = _bias_tables(s)

    part, imp = _attn1(qT, rows(ckf), cols(cvf), rows(kw), cols(vw), gatesT, t_c, t_w, _overlap_T(s))
    imp_t = imp.reshape(b, kv, nqt // TOPK_TILES, TOPK_TILES, ns, tq).transpose(0, 1, 2, 4, 3, 5)
    sel_t, lst = _topk(imp_t)
    return _attn2(lst, qT, rows(ks), cols(vs), sel_t, gatesT, t_s, srow, part)


HALO = 8
CONV_SLAB = 512


def _softplus(x):
    return jnp.maximum(x, 0.0) + jnp.log1p(jnp.exp(-jnp.abs(x)))


def _ssd_kernel(raw_ref, z_ref, dt_ref, dtT_ref, cw_ref, cb_ref, dtb_ref, dtbT_ref,
                al_ref, alT_ref, dsk_ref, nw_ref, o_ref, st_ref, tail_ref, ext_ref, xc_ref, y_ref):
    q = SSM_CHUNK

    @pl.when(pl.program_id(1) == 0)
    def _():
        st_ref[...] = jnp.zeros(st_ref.shape, F32)
        tail_ref[...] = jnp.zeros(tail_ref.shape, F32)

    ext_ref[0:HALO, :] = tail_ref[...]
    ext_ref[HALO:HALO + q, :] = raw_ref[0].astype(F32)
    tail_ref[...] = ext_ref[q:q + HALO, :]
    for c0 in range(0, CONV_DIM, CONV_SLAB):
        sl = slice(c0, c0 + CONV_SLAB)
        y = cb_ref[:, sl]
        for k in range(CONV_WIDTH):
            off = HALO - (CONV_WIDTH - 1) + k
            y = y + cw_ref[k:k + 1, sl] * ext_ref[off:off + q, sl]
        xc_ref[:, sl] = y * jax.nn.sigmoid(y)

    dt = _softplus(dt_ref[0][:, :SSM_HEADS] + dtb_ref[...])
    dtT = _softplus(dtT_ref[0] + dtbT_ref[...])
    ri = lax.broadcasted_iota(jnp.int32, (q, q), 0)
    ci = lax.broadcasted_iota(jnp.int32, (q, q), 1)
    tri = ri >= ci
    low = jnp.where(tri, 1.0, 0.0).astype(BF16)
    upp = jnp.where(ri <= ci, 1.0, 0.0).astype(BF16)
    cum = _dot01_l(low, dt * (-jnp.exp(al_ref[...])))
    cumT = _dot01_r(dtT * (-jnp.exp(alT_ref[...])), upp)
    ecum = jnp.exp(cum)
    lane_lo = lax.broadcasted_iota(jnp.int32, (1, LANES), 1) < SSM_HEAD_DIM
    b0 = SSM_INNER
    c0 = SSM_INNER + SSM_GROUPS * SSM_STATE

    for g in range(SSM_GROUPS):
        bgf = xc_ref[:, b0 + g * SSM_STATE:b0 + (g + 1) * SSM_STATE]
        cg = xc_ref[:, c0 + g * SSM_STATE:c0 + (g + 1) * SSM_STATE].astype(BF16)
        cb = lax.dot_general(cg, bgf.astype(BF16), (((1,), (1,)), ((), ())), preferred_element_type=F32)
        bTg = bgf.T
        for pr in range(SSM_HPG // 2):
            hp = g * (SSM_HPG // 2) + pr
            xp = xc_ref[:, hp * LANES:(hp + 1) * LANES].astype(BF16)
            ys, sts, ecs, els = [], [], [], []
            for k in range(2):
                h = 2 * hp + k
                crow = cumT[h:h + 1, :]
                seg = cum[:, h:h + 1] - crow
                w = cb * jnp.exp(jnp.where(tri, seg, NEG)) * dtT[h:h + 1, :]
                ys.append(jnp.dot(w.astype(BF16), xp, preferred_element_type=F32))
                clast = crow[:, q - 1:q]
                to_end = jnp.exp(clast - crow) * dtT[h:h + 1, :]
                sts.append(jnp.dot((bTg * to_end).astype(BF16), xp, preferred_element_type=F32))
                ecs.append(ecum[:, h:h + 1])
                els.append(jnp.exp(clast))
            st = st_ref[hp]
            y_in = jnp.dot(cg, st.astype(BF16), preferred_element_type=F32)
            y_ref[:, hp * LANES:(hp + 1) * LANES] = (
                jnp.where(lane_lo, ys[0], ys[1]) + y_in * jnp.where(lane_lo, ecs[0], ecs[1]))
            st_ref[hp] = st * jnp.where(lane_lo, els[0], els[1]) + jnp.where(lane_lo, sts[0], sts[1])

    gw = SSM_INNER // SSM_GROUPS
    for g in range(SSM_GROUPS):
        sl = slice(g * gw, (g + 1) * gw)
        z = z_ref[0][:, sl].astype(F32)
        y = (y_ref[:, sl] + dsk_ref[:, sl] * xc_ref[:, sl]) * (z * jax.nn.sigmoid(z))
        y = y * lax.rsqrt(jnp.mean(y * y, axis=-1, keepdims=True) + EPS)
        o_ref[0, :, sl] = (y * nw_ref[:, sl]).astype(o_ref.dtype)


def _ssd(xbc, z, dts, conv_w, conv_b, dt_bias, a_log, d_skip, ssm_norm):
    b, s, _ = xbc.shape
    q = SSM_CHUNK
    hh = SSM_HEADS
    dtT = dts[..., :hh].transpose(0, 2, 1)
    row = lambda v: v.reshape(1, -1).astype(F32)
    col = lambda v: v.reshape(-1, 1).astype(F32)
    const = lambda shape: pl.BlockSpec(shape, lambda bi, c: (0, 0))
    return pl.pallas_call(
        _ssd_kernel,
        grid=(b, s // q),
        in_specs=[pl.BlockSpec((1, q, CONV_DIM), lambda bi, c: (bi, c, 0)),
                  pl.BlockSpec((1, q, SSM_INNER), lambda bi, c: (bi, c, 0)),
                  pl.BlockSpec((1, q, LANES), lambda bi, c: (bi, c, 0)),
                  pl.BlockSpec((1, hh, q), lambda bi, c: (bi, 0, c)),
                  const((CONV_WIDTH, CONV_DIM)), const((1, CONV_DIM)),
                  const((1, hh)), const((hh, 1)), const((1, hh)), const((hh, 1)),
                  const((1, SSM_INNER)), const((1, SSM_INNER))],
        out_specs=pl.BlockSpec((1, q, SSM_INNER), lambda bi, c: (bi, c, 0)),
        out_shape=jax.ShapeDtypeStruct((b, s, SSM_INNER), BF16),
        scratch_shapes=[pltpu.VMEM((hh // 2, SSM_STATE, LANES), F32),
                        pltpu.VMEM((HALO, CONV_DIM), F32),
                        pltpu.VMEM((q + HALO, CONV_DIM), F32),
                        pltpu.VMEM((q, CONV_DIM), F32),
                        pltpu.VMEM((q, SSM_INNER), F32)],
        compiler_params=_cparams(("parallel", "arbitrary")),
        name="ssd",
    )(xbc, z, dts, dtT, conv_w, row(conv_b), row(dt_bias), col(dt_bias), row(a_log), col(a_log),
      row(jnp.repeat(d_skip, SSM_HEAD_DIM)), row(ssm_norm))


MERGE_TM = 512


def _merge_kernel(h_ref, y_ref, ym_ref, mg_ref, waT_ref, ws_ref, wo_ref, post_ref, o_ref):
    d = h_ref.shape[1]
    tq = y_ref.shape[4] // NSA_GROUP
    yT = jnp.concatenate(
        [jnp.concatenate([y_ref[0, kv, t, :, h * tq:(h + 1) * tq]
                          for kv in range(NSA_KV_HEADS) for h in range(NSA_GROUP)], axis=0)
         for t in range(y_ref.shape[2])], axis=1)
    a = jnp.dot(waT_ref[...], yT, preferred_element_type=F32).T
    m = jnp.dot(ym_ref[...], ws_ref[...], preferred_element_type=F32)
    gts = jax.nn.sigmoid(mg_ref[...].astype(F32))
    merged = gts[:, :d] * a + gts[:, d:] * m
    out = jnp.dot(merged.astype(BF16), wo_ref[...], preferred_element_type=F32)
    o_ref[...] = h_ref[...] + _rms(out, post_ref[...])


def _merge(h2d, ya_tiles, ym, mg, wa, ws, wo, post):
    t, d = h2d.shape
    b, kv, nqt, dh, n = ya_tiles.shape
    tq = n // NSA_GROUP
    s = nqt * tq
    tm = min(MERGE_TM, s)
    per_seq = s // tm
    const = lambda shape: pl.BlockSpec(shape, lambda i: (0, 0), pipeline_mode=pl.Buffered(1))
    rows = lambda c: pl.BlockSpec((tm, c), lambda i: (i, 0))
    return pl.pallas_call(
        _merge_kernel,
        grid=(t // tm,),
        in_specs=[rows(d),
                  pl.BlockSpec((1, kv, tm // tq, dh, n), lambda i: (i // per_seq, 0, i % per_seq, 0, 0)),
                  rows(SSM_INNER), rows(2 * d),
                  const((d, NSA_WIDTH)), const((SSM_INNER, d)), const((d, d)), const((1, d))],
        out_specs=rows(d),
        out_shape=jax.ShapeDtypeStruct((t, d), F32),
        compiler_params=_cparams(("parallel",)),
        name="merge",
    )(h2d, ya_tiles, ym, mg, wa.T.astype(BF16), ws.astype(BF16), wo.astype(BF16), post.reshape(1, d))


def _mixer(h2d, b, s, mix_pre, w_in, cmp_pos, w_ck1, w_ck2, w_cv1, w_cv2, conv_w, conv_b, dt_bias,
           a_log, d_skip, ssm_norm, w_attn_branch, w_ssm_branch, w_out, mix_post):
    o = 0
    cols = {}
    for name, width in (("q", NSA_WIDTH), ("kv", 6 * KV_WIDTH), ("ng", NSA_HEADS * 3), ("z", SSM_INNER),
                        ("xbc", CONV_DIM), ("dt", SSM_HEADS), ("mg", 2 * D_MODEL)):
        cols[name] = w_in[:, o:o + width]
        o += width
    pad = LANES - SSM_HEADS - NSA_HEADS * 3
    w_small = jnp.concatenate([cols["dt"], cols["ng"], jnp.zeros((D_MODEL, pad), F32)], axis=1)

    u = _rmsnorm(h2d, mix_pre)
    qT = _proj_q(u, cols["q"], b, s, HEAD_DIM ** -0.5 * LOG2E, ATT_TQ)
    kvs = _proj(u, cols["kv"], BF16).reshape(b, s, 6 * KV_WIDTH)
    z = _proj(u, cols["z"], BF16).reshape(b, s, SSM_INNER)
    xbc = _proj(u, cols["xbc"], BF16).reshape(b, s, CONV_DIM)
    mg = _proj(u, cols["mg"], BF16)
    small = _proj(u, w_small, F32).reshape(b, s, LANES)
    ng = small[..., SSM_HEADS:SSM_HEADS + NSA_HEADS * 3]

    y_a = _nsa(qT, kvs, ng, cmp_pos, w_ck1, w_ck2, w_cv1, w_cv2)
    y_m = _ssd(xbc, z, small, conv_w, conv_b, dt_bias, a_log, d_skip, ssm_norm)
    return _merge(h2d, y_a, y_m.reshape(b * s, SSM_INNER), mg,
                  w_attn_branch, w_ssm_branch, w_out, mix_post)


def kernel(x, ffn1_pre, ffn1_gate, ffn1_up, ffn1_down, ffn1_post, mix_pre, w_in, cmp_pos, w_ck1, w_ck2,
           w_cv1, w_cv2, conv_w, conv_b, dt_bias, a_log, d_skip, ssm_norm, w_attn_branch, w_ssm_branch,
           w_out, mix_post, ffn2_pre, ffn2_gate, ffn2_up, ffn2_down, ffn2_post):
    b, s, d = x.shape
    h = x.reshape(b * s, d)
    for l in range(ffn1_pre.shape[0]):
        h = _ffn(h, ffn1_pre[l], ffn1_gate[l], ffn1_up[l], ffn1_down[l], ffn1_post[l])
        h = _mixer(h, b, s, mix_pre[l], w_in[l], cmp_pos[l], w_ck1[l], w_ck2[l], w_cv1[l], w_cv2[l],
                   conv_w[l], conv_b[l], dt_bias[l], a_log[l], d_skip[l], ssm_norm[l],
                   w_attn_branch[l], w_ssm_branch[l], w_out[l], mix_post[l])
        h = _ffn(h, ffn2_pre[l], ffn2_gate[l], ffn2_up[l], ffn2_down[l], ffn2_post[l])
    return h.reshape(b, s, d)
```

```python
import functools

import numpy as np
import jax
import jax.numpy as jnp
from jax import lax
from jax.experimental import pallas as pl
from jax.experimental.pallas import tpu as pltpu

F32 = jnp.float32
BF16 = jnp.bfloat16

D_MODEL = 1024
NSA_HEADS = 16
NSA_KV_HEADS = 4
NSA_GROUP = NSA_HEADS // NSA_KV_HEADS
HEAD_DIM = 64
CMP_LEN = 32
CMP_STRIDE = 16
CMP_HIDDEN = 2 * HEAD_DIM
SEL_BLOCK = 64
SEL_TOPK = 16
WINDOW = 512
NSA_WIDTH = NSA_HEADS * HEAD_DIM
KV_WIDTH = NSA_KV_HEADS * HEAD_DIM
BIG = 1e4
NEG = -1e30

SSM_INNER = 2 * D_MODEL
SSM_HEAD_DIM = 64
SSM_HEADS = SSM_INNER // SSM_HEAD_DIM
SSM_GROUPS = 4
SSM_HPG = SSM_HEADS // SSM_GROUPS
SSM_STATE = 128
CONV_WIDTH = 4
SSM_CHUNK = 256
CONV_DIM = SSM_INNER + 2 * SSM_GROUPS * SSM_STATE
FFN_HIDDEN = 2816
EPS = 1e-6
LOG2E = 1.4426950408889634

LANES = 128
VMEM_LIMIT = 56 * 1024 * 1024


def _cparams(sem):
    return pltpu.CompilerParams(dimension_semantics=sem, vmem_limit_bytes=VMEM_LIMIT)


def _rms(x, g):
    return x * lax.rsqrt(jnp.mean(x * x, axis=-1, keepdims=True) + EPS) * g


def _split3(x):
    hi = x.astype(BF16)
    r1 = x - hi.astype(F32)
    mid = r1.astype(BF16)
    lo = (r1 - mid.astype(F32)).astype(BF16)
    return hi, mid, lo


def _dot01_l(a01, x):
    hi, mid, lo = _split3(x)
    d = lambda v: jnp.dot(a01, v, preferred_element_type=F32)
    return d(hi) + (d(mid) + d(lo))


def _dot01_r(x, a01):
    hi, mid, lo = _split3(x)
    d = lambda v: jnp.dot(v, a01, preferred_element_type=F32)
    return d(hi) + (d(mid) + d(lo))


def _interleave(gens):
    live = list(gens)
    while live:
        live = [g for g in live if next(g, StopIteration) is not StopIteration]


FFN_TM = 256
FFN_HC = 256
FFN_TILES = 4


def _ffn_tile(t, x_ref, pre_ref, wg_ref, wu_ref, wd_ref, post_ref, o_ref):
    rows = pl.ds(t * FFN_TM, FFN_TM)
    x = x_ref[rows, :]
    ub = _rms(x, pre_ref[...]).astype(BF16)
    yield
    acc = jnp.zeros(x.shape, F32)
    for c in range(FFN_HIDDEN // FFN_HC):
        sl = slice(c * FFN_HC, (c + 1) * FFN_HC)
        g = jnp.dot(ub, wg_ref[:, sl], preferred_element_type=F32)
        u = jnp.dot(ub, wu_ref[:, sl], preferred_element_type=F32)
        a = (g * jax.nn.sigmoid(g)) * u
        acc = acc + jnp.dot(a.astype(BF16), wd_ref[sl, :], preferred_element_type=F32)
        yield
    o_ref[rows, :] = x + 0.5 * _rms(acc, post_ref[...])
    yield


def _ffn_kernel(*refs):
    _interleave([_ffn_tile(t, *refs) for t in range(FFN_TILES)])


def _ffn(h2d, pre, wg, wu, wd, post):
    t, d = h2d.shape
    tm = min(FFN_TM * FFN_TILES, t)
    const = lambda shape: pl.BlockSpec(shape, lambda i: (0, 0), pipeline_mode=pl.Buffered(1))
    return pl.pallas_call(
        _ffn_kernel,
        grid=(t // tm,),
        in_specs=[pl.BlockSpec((tm, d), lambda i: (i, 0)),
                  const((1, d)), const((d, FFN_HIDDEN)), const((d, FFN_HIDDEN)),
                  const((FFN_HIDDEN, d)), const((1, d))],
        out_specs=pl.BlockSpec((tm, d), lambda i: (i, 0)),
        out_shape=jax.ShapeDtypeStruct((t, d), F32),
        compiler_params=_cparams(("parallel",)),
        name="ffn",
    )(h2d, pre.reshape(1, d), wg.astype(BF16), wu.astype(BF16), wd.astype(BF16), post.reshape(1, d))


PROJ_TM = 1024
PROJ_TN = 1024


def _rmsnorm_kernel(x_ref, g_ref, o_ref):
    o_ref[...] = _rms(x_ref[...], g_ref[...]).astype(o_ref.dtype)


def _rmsnorm(h2d, g):
    t, d = h2d.shape
    tm = min(PROJ_TM, t)
    return pl.pallas_call(
        _rmsnorm_kernel,
        grid=(t // tm,),
        in_specs=[pl.BlockSpec((tm, d), lambda i: (i, 0)), pl.BlockSpec((1, d), lambda i: (0, 0))],
        out_specs=pl.BlockSpec((tm, d), lambda i: (i, 0)),
        out_shape=jax.ShapeDtypeStruct((t, d), BF16),
        compiler_params=_cparams(("parallel",)),
        name="rmsnorm",
    )(h2d, g.reshape(1, d))


def _proj_kernel(u_ref, w_ref, o_ref, *, scale):
    acc = jnp.dot(u_ref[...], w_ref[...], preferred_element_type=F32)
    if scale != 1.0:
        acc = acc * scale
    o_ref[...] = acc.astype(o_ref.dtype)


def _proj(u, w, out_dtype, scale=1.0):
    t, d = u.shape
    n = w.shape[1]
    tm = min(PROJ_TM, t)
    tn = next(c for c in range(min(PROJ_TN, n), 0, -LANES) if n % c == 0)
    return pl.pallas_call(
        functools.partial(_proj_kernel, scale=scale),
        grid=(t // tm, n // tn),
        in_specs=[pl.BlockSpec((tm, d), lambda i, j: (i, 0)),
                  pl.BlockSpec((d, tn), lambda i, j: (0, j))],
        out_specs=pl.BlockSpec((tm, tn), lambda i, j: (i, j)),
        out_shape=jax.ShapeDtypeStruct((t, n), out_dtype),
        compiler_params=_cparams(("parallel", "parallel")),
        name="proj",
    )(u, w.astype(BF16))


def _proj_q_kernel(u_ref, wT_ref, o_ref, *, scale, tq):
    r = lax.dot_general(wT_ref[...], u_ref[...], (((1,), (1,)), ((), ())), preferred_element_type=F32)
    r = (r * scale).astype(o_ref.dtype)
    for g in range(NSA_KV_HEADS):
        for t in range(u_ref.shape[0] // tq):
            for h in range(NSA_GROUP):
                r0 = (g * NSA_GROUP + h) * HEAD_DIM
                o_ref[0, g, t, :, h * tq:(h + 1) * tq] = r[r0:r0 + HEAD_DIM, t * tq:(t + 1) * tq]


def _proj_q(u, w, b, s, scale, tq):
    t, d = u.shape
    tm = min(PROJ_TM, s)
    per_seq = s // tm
    return pl.pallas_call(
        functools.partial(_proj_q_kernel, scale=scale, tq=tq),
        grid=(t // tm,),
        in_specs=[pl.BlockSpec((tm, d), lambda i: (i, 0)),
                  pl.BlockSpec((NSA_WIDTH, d), lambda i: (0, 0))],
        out_specs=pl.BlockSpec((1, NSA_KV_HEADS, tm // tq, HEAD_DIM, NSA_GROUP * tq),
                               lambda i: (i // per_seq, 0, i % per_seq, 0, 0)),
        out_shape=jax.ShapeDtypeStruct((b, NSA_KV_HEADS, s // tq, HEAD_DIM, NSA_GROUP * tq), BF16),
        compiler_params=_cparams(("parallel",)),
        name="proj_q",
    )(u, w.T.astype(BF16))


def _compress_kernel(r_ref, pt_ref, pb_ref, w1t_ref, w1b_ref, w2_ref, o_ref):
    r = r_ref[0].astype(F32)
    top = jnp.dot((r + pt_ref[...]).astype(BF16), w1t_ref[...], preferred_element_type=F32)
    bot = jnp.dot((r + pb_ref[...]).astype(BF16), w1b_ref[...], preferred_element_type=F32)
    ncp = r.shape[0]
    h = top + pltpu.roll(bot, ncp - 1, 0)
    a = (h * jax.nn.sigmoid(h)).astype(BF16)
    o_ref[0] = jnp.dot(a, w2_ref[...], preferred_element_type=F32).astype(o_ref.dtype)


def _compress(k, pos, w1, w2):
    b, s, _ = k.shape
    ncp = s // CMP_STRIDE
    half = CMP_LEN // 2
    rw = half * KV_WIDTH
    eye = jnp.eye(NSA_KV_HEADS, dtype=F32)
    w1r = w1.reshape(CMP_LEN, HEAD_DIM, CMP_HIDDEN)

    def big(wpart):
        return jnp.einsum('ldj,gk->lgdkj', wpart, eye).reshape(rw, NSA_KV_HEADS * CMP_HIDDEN).astype(BF16)

    def posrow(p):
        return jnp.broadcast_to(p[:, None, :], (half, NSA_KV_HEADS, HEAD_DIM)).reshape(1, rw)

    w2big = jnp.einsum('jd,gk->gjkd', w2, eye).reshape(NSA_KV_HEADS * CMP_HIDDEN, KV_WIDTH).astype(BF16)
    nh = NSA_KV_HEADS * CMP_HIDDEN
    const = lambda shape: pl.BlockSpec(shape, lambda i: (0, 0))
    return pl.pallas_call(
        _compress_kernel,
        grid=(b,),
        in_specs=[pl.BlockSpec((1, ncp, rw), lambda i: (i, 0, 0)),
                  const((1, rw)), const((1, rw)), const((rw, nh)), const((rw, nh)),
                  const((nh, KV_WIDTH))],
        out_specs=pl.BlockSpec((1, ncp, KV_WIDTH), lambda i: (i, 0, 0)),
        out_shape=jax.ShapeDtypeStruct((b, ncp, KV_WIDTH), BF16),
        compiler_params=_cparams(("parallel",)),
        name="compress",
    )(k.reshape(b, ncp, rw), posrow(pos[:half]), posrow(pos[half:]), big(w1r[:half]), big(w1r[half:]), w2big)


ATT_TQ = 128
SUPER = 2 * SEL_BLOCK
SEL_UNROLL = 5
LIST_LANES = LANES
TOPK_TILES = 8
ATT_PAIR = 8


def _softmax_steps(states, blocks_list):
    n = states[0][0].shape[1]
    ms, ls, accs = [], [], []
    for (m, l8, acc), blocks in zip(states, blocks_list):
        part = None
        for s, rows, _ in blocks:
            hr = s.shape[0] // len(rows)
            for a, r in enumerate(rows):
                c = jnp.max(s[a * hr:(a + 1) * hr].reshape(hr // 8, 8, n), axis=0) + r
                part = c if part is None else jnp.maximum(part, c)
        m_new = jnp.maximum(m, jnp.max(part, axis=0, keepdims=True))
        alpha = jnp.exp2(m - m_new)
        ms.append(m_new)
        ls.append(l8 * alpha)
        accs.append(acc * alpha)
    for u in range(max(len(blocks) for blocks in blocks_list)):
        for k, blocks in enumerate(blocks_list):
            if u >= len(blocks):
                continue
            s, rows, vT = blocks[u]
            hr = s.shape[0] // len(rows)
            ps = [jnp.exp2(s[a * hr:(a + 1) * hr] - (ms[k] - r)) for a, r in enumerate(rows)]
            p = ps[0] if len(ps) == 1 else jnp.concatenate(ps, axis=0)
            ls[k] = ls[k] + jnp.sum(p.reshape(p.shape[0] // 8, 8, n), axis=0)
            accs[k] = accs[k] + jnp.dot(vT, p.astype(BF16), preferred_element_type=F32)
    return [(ms[k], ls[k], accs[k]) for k in range(len(states))]


def _softmax_step(state, blocks):
    return _softmax_steps([state], [blocks])[0]


def _softmax_init(n):
    return (jnp.full((1, n), NEG, F32), jnp.zeros((8, n), F32), jnp.zeros((HEAD_DIM, n), F32))


def _attn1_tile(t, i, qT_ref, ck_ref, cvT_ref, kw_ref, vwT_ref, gates_ref, tc_ref, tw_ref, ovT_ref,
                part_ref, imp_ref, tq, ncp, rows):
    q0 = i * tq
    n = NSA_GROUP * tq
    qT = qT_ref[0, 0, t]

    n_kt = WINDOW // tq + 1
    blocks = []
    for kt in range(n_kt):
        k0 = q0 - WINDOW + kt * tq
        pen = jnp.where(k0 < 0, NEG, 0.0)
        k0 = pl.multiple_of(jnp.maximum(k0, 0), tq)
        sw = jnp.dot(kw_ref[0, 0, pl.ds(k0, tq), :], qT, preferred_element_type=F32) + \
            tw_ref[0, kt * tq:(kt + 1) * tq, :]
        blocks.append((sw, [pen], vwT_ref[0, 0, :, pl.ds(k0, tq)]))
    yield

    start = pl.multiple_of(ncp - i * (tq // CMP_STRIDE), 8)
    s = jnp.dot(ck_ref[0, 0, 0:rows, :], qT, preferred_element_type=F32) + tc_ref[0, pl.ds(start, rows), :]
    yield
    e = jnp.exp2(s - jnp.max(s, axis=0, keepdims=True))
    yield
    t_rel = jnp.bitwise_and(lax.broadcasted_iota(jnp.int32, (1, n), 1), tq - 1)
    inv = jnp.where(q0 + t_rel >= CMP_LEN - 1, 1.0 / jnp.sum(e, axis=0, keepdims=True), 0.0)
    p = e * inv
    o_c = jnp.dot(cvT_ref[0, 0, :, 0:rows], p.astype(BF16), preferred_element_type=F32)
    yield
    psum = p[:, 0:tq]
    for h in range(1, NSA_GROUP):
        psum = psum + p[:, h * tq:(h + 1) * tq]
    imp_ref[0, 0, t] = _dot01_l(ovT_ref[:, 0:rows], psum)
    yield

    _, l_w, acc_w = _softmax_step(_softmax_init(n), blocks)
    l_w = jnp.sum(l_w, axis=0, keepdims=True)
    yield

    gsig = jax.nn.sigmoid(gates_ref[0, 0, t])
    part_ref[0, 0, t] = gsig[0:1, :] * o_c + gsig[2:3, :] * (acc_w * (1.0 / l_w))
    yield


def _attn1_kernel(*refs, tq, ncp):
    step = pl.program_id(2)
    i0 = step * ATT_PAIR
    half = pl.num_programs(2) // 2

    def run(rows):
        _interleave([_attn1_tile(t, i0 + t, *refs, tq, ncp, rows) for t in range(ATT_PAIR)])

    @pl.when(step < half)
    def _():
        run(ncp // 2)

    @pl.when(step >= half)
    def _():
        run(ncp)


def _attn1(qT, ck, cvT, kw, vwT, gatesT, t_c, t_w, ovT):
    b, kv, nqt, dh, n = qT.shape
    tq = n // NSA_GROUP
    s = nqt * tq
    ncp = ck.shape[2]
    ns = s // SEL_BLOCK
    kern = functools.partial(_attn1_kernel, tq=tq, ncp=ncp)
    tile = lambda r, c: pl.BlockSpec((1, 1, ATT_PAIR, r, c), lambda bi, g, i: (bi, g, i, 0, 0))
    per_bg = lambda r, c: pl.BlockSpec((1, 1, r, c), lambda bi, g, i: (bi, g, 0, 0))
    per_g = lambda r: pl.BlockSpec((1, r, n), lambda bi, g, i: (g, 0, 0))
    return pl.pallas_call(
        kern,
        grid=(b, kv, nqt // ATT_PAIR),
        in_specs=[tile(dh, n), per_bg(ncp, dh), per_bg(dh, ncp), per_bg(s, dh), per_bg(dh, s), tile(3, n),
                  per_g(2 * ncp), per_g(WINDOW + tq),
                  pl.BlockSpec((ns, ncp), lambda bi, g, i: (0, 0))],
        out_specs=[tile(dh, n), tile(ns, tq)],
        out_shape=[jax.ShapeDtypeStruct((b, kv, nqt, dh, n), F32),
                   jax.ShapeDtypeStruct((b, kv, nqt, ns, tq), F32)],
        compiler_params=_cparams(("parallel", "parallel", "arbitrary")),
        name="attn1",
    )(qT, ck, cvT, kw, vwT, gatesT, t_c, t_w, ovT)


def _topk_kernel(imp_ref, pair_ref, low_ref, pick_ref, sel_ref, lst_ref, work_ref, *, ns, n_sel, tq):
    c = pl.program_id(2)
    shape = (ns, TOPK_TILES, tq)
    j = lax.broadcasted_iota(jnp.int32, shape, 0)
    jf = j.astype(F32)
    t = (c * TOPK_TILES + lax.broadcasted_iota(jnp.int32, shape, 1)) * tq + \
        lax.broadcasted_iota(jnp.int32, shape, 2)
    cur = jnp.right_shift(t, SEL_BLOCK.bit_length() - 1)
    valid = j <= cur
    forced = (j == 0) | (j == cur) | (j == cur - 1)
    work_ref[...] = jnp.where(forced, -jnp.inf, jnp.where(valid, imp_ref[0, 0, 0], -BIG))
    chunk = 8

    def one_round(_, carry):
        best_v = best_k = None
        for c0 in range(0, ns, chunk):
            v = work_ref[c0:c0 + chunk]
            k = (lax.broadcasted_iota(jnp.int32, (chunk, TOPK_TILES, tq), 0) + c0).astype(F32)
            while v.shape[0] > 1:
                h = v.shape[0] // 2
                hi = v[h:] > v[:h]
                v = jnp.where(hi, v[h:], v[:h])
                k = jnp.where(hi, k[h:], k[:h])
            if best_v is None:
                best_v, best_k = v, k
            else:
                hi = v > best_v
                best_v = jnp.where(hi, v, best_v)
                best_k = jnp.where(hi, k, best_k)
        for c0 in range(0, ns, chunk):
            k = (lax.broadcasted_iota(jnp.int32, (chunk, TOPK_TILES, tq), 0) + c0).astype(F32)
            work_ref[c0:c0 + chunk] = jnp.where(k == best_k, -jnp.inf, work_ref[c0:c0 + chunk])
        return carry

    lax.fori_loop(0, n_sel - 3, one_round, 0)
    selm = jnp.where(valid & (work_ref[...] == -jnp.inf), 1.0, 0.0)
    sel_ref[0, 0, 0] = selm

    nsb = ns // 2
    any_q = jnp.broadcast_to(jnp.max(selm, axis=2, keepdims=True), shape[:2] + (LIST_LANES,))
    any_q = any_q.reshape(ns * TOPK_TILES, LIST_LANES).astype(BF16)
    rows = lax.broadcasted_iota(jnp.int32, (nsb * TOPK_TILES, 1), 0)
    sb = jnp.right_shift(rows, 3)
    tile = c * TOPK_TILES + jnp.bitwise_and(rows, TOPK_TILES - 1)
    flag = jnp.dot(pair_ref[...], any_q, preferred_element_type=F32)
    flag = jnp.where((flag > 0.0) & (sb < tile), 1.0, 0.0)
    rank = jnp.dot(low_ref[...], flag.astype(BF16), preferred_element_type=F32)
    k_row = lax.broadcasted_iota(jnp.int32, (1, LIST_LANES), 1)
    hit = jnp.where((rank == (k_row + 1).astype(F32)) & (flag > 0.0), 1.0, 0.0).astype(BF16)
    lst = jnp.dot(pick_ref[...], hit, preferred_element_type=F32)
    total = rank[(nsb - 1) * TOPK_TILES:nsb * TOPK_TILES, :]
    lst_ref[0, 0, 0] = jnp.where(k_row == LIST_LANES - 1, total, lst).astype(jnp.int32)


def _topk(imp_t):
    b, kv, nc, ns, tiles, tq = imp_t.shape
    nsb = ns // 2
    r = np.arange(nsb * tiles)
    q = np.arange(ns * tiles)
    same = (r[:, None] % tiles) == (q[None, :] % tiles)
    pair = same & ((q[None, :] // tiles) // 2 == (r[:, None] // tiles))
    low = ((r[:, None] % tiles) == (r[None, :] % tiles)) & ((r[None, :] // tiles) <= (r[:, None] // tiles))
    pick = (np.arange(tiles)[:, None] == (r[None, :] % tiles)) * (r[None, :] // tiles)
    const = lambda a: jnp.asarray(a.astype(np.float32), dtype=BF16)
    whole = lambda a: pl.BlockSpec(a.shape, lambda bi, g, c: (0, 0))
    blk = pl.BlockSpec((1, 1, 1, ns, tiles, tq), lambda bi, g, c: (bi, g, c, 0, 0, 0))
    return pl.pallas_call(
        functools.partial(_topk_kernel, ns=ns, n_sel=min(SEL_TOPK, ns), tq=tq),
        grid=(b, kv, nc),
        in_specs=[blk, whole(pair), whole(low), whole(pick)],
        out_specs=[blk, pl.BlockSpec((1, 1, 1, tiles, LIST_LANES), lambda bi, g, c: (bi, g, c, 0, 0))],
        out_shape=[jax.ShapeDtypeStruct(imp_t.shape, F32),
                   jax.ShapeDtypeStruct((b, kv, nc, tiles, LIST_LANES), jnp.int32)],
        scratch_shapes=[pltpu.VMEM((ns, tiles, tq), F32)],
        compiler_params=_cparams(("parallel", "parallel", "arbitrary")),
        name="topk",
    )(imp_t, const(pair), const(low), const(pick))


def _attn2_kernel(lst_ref, qT_ref, ks_ref, vsT_ref, sel_ref, gates_ref, ts_ref, srow_ref, part_ref,
                  o_ref, *, tq):
    n = NSA_GROUP * tq
    i0 = pl.program_id(2) * ATT_PAIR
    srow = srow_ref[0]
    tiles = [(t, i0 + t, jnp.bitwise_and(i0 + t, TOPK_TILES - 1)) for t in range(ATT_PAIR)]
    cnts = [lst_ref[0, 0, 0, a, LIST_LANES - 1] for _, _, a in tiles]

    def sel_rows(a, sb, base):
        out = []
        for half in range(2):
            row = jnp.where(sel_ref[0, 0, 0, 2 * sb + half, pl.ds(a, 1), :] > 0.0, 0.0, NEG)
            out.append(jnp.concatenate([row] * NSA_GROUP, axis=1) + base)
        return out

    def scores(t, sb, table):
        k0 = pl.multiple_of(sb * SUPER, SUPER)
        s = jnp.dot(ks_ref[0, 0, pl.ds(k0, SUPER), :], qT_ref[0, 0, t], preferred_element_type=F32) + table
        return s, vsT_ref[0, 0, :, pl.ds(k0, SUPER)]

    def listed(tile, k):
        t, i, a = tile
        sb = lst_ref[0, 0, 0, a, k]
        pen = jnp.where(k < cnts[t], 0.0, NEG)
        base = srow * (sb * SUPER - i * tq).astype(F32) + pen
        s, vT = scores(t, sb, ts_ref[0, 0])
        return s, sel_rows(a, sb, base), vT

    def diagonal(tile):
        t, i, a = tile
        s, vT = scores(t, i, ts_ref[0, 1])
        return s, sel_rows(a, i, 0.0), vT

    def group(first_k, with_diag):
        blocks_list = [[diagonal(tile)] if with_diag else [] for tile in tiles]
        for u in range(SEL_UNROLL - with_diag):
            for tile in tiles:
                blocks_list[tile[0]].append(listed(tile, first_k + u))
        return blocks_list

    states = _softmax_steps([_softmax_init(n)] * ATT_PAIR, group(0, 1))

    def body(it, sts):
        return tuple(_softmax_steps(list(sts), group(SEL_UNROLL - 1 + it * SEL_UNROLL, 0)))

    rest = functools.reduce(jnp.maximum, cnts) - (SEL_UNROLL - 1)
    n_it = (jnp.maximum(rest, 0) + SEL_UNROLL - 1) // SEL_UNROLL
    states = lax.fori_loop(0, n_it, body, tuple(states))
    for t, _, _ in tiles:
        _, l_s, acc_s = states[t]
        l_s = jnp.sum(l_s, axis=0, keepdims=True)
        gsig = jax.nn.sigmoid(gates_ref[0, 0, t])
        o_ref[0, 0, t] = (part_ref[0, 0, t] + gsig[1:2, :] * (acc_s * (1.0 / l_s))).astype(o_ref.dtype)


def _attn2(lst, qT, ks, vsT, sel_t, gatesT, t_s, srow, part):
    b, kv, nqt, dh, n = qT.shape
    tq = n // NSA_GROUP
    s = nqt * tq
    ns = s // SEL_BLOCK
    per_top = TOPK_TILES // ATT_PAIR
    tile = lambda r: pl.BlockSpec((1, 1, ATT_PAIR, r, n), lambda bi, g, i: (bi, g, i, 0, 0))
    per_bg = lambda r, c: pl.BlockSpec((1, 1, r, c), lambda bi, g, i: (bi, g, 0, 0))
    return pl.pallas_call(
        functools.partial(_attn2_kernel, tq=tq),
        grid=(b, kv, nqt // ATT_PAIR),
        in_specs=[pl.BlockSpec((1, 1, 1, TOPK_TILES, LIST_LANES), lambda bi, g, i: (bi, g, i // per_top, 0, 0),
                               memory_space=pltpu.SMEM),
                  tile(dh), per_bg(s, dh), per_bg(dh, s),
                  pl.BlockSpec((1, 1, 1, ns, TOPK_TILES, tq),
                               lambda bi, g, i: (bi, g, i // per_top, 0, 0, 0)),
                  tile(3),
                  pl.BlockSpec((1, 2, SUPER, n), lambda bi, g, i: (g, 0, 0, 0)),
                  pl.BlockSpec((1, 1, n), lambda bi, g, i: (g, 0, 0)),
                  tile(dh)],
        out_specs=tile(dh),
        out_shape=jax.ShapeDtypeStruct((b, kv, nqt, dh, n), BF16),
        compiler_params=_cparams(("parallel", "parallel", "arbitrary")),
        name="attn2",
    )(lst, qT, ks, vsT, sel_t, gatesT, t_s, srow, part)


def _alibi_slopes(n):
    return np.array([2.0 ** (-8.0 * (h + 1) / n) for h in range(n)], dtype=np.float32)


def _overlap_T(s):
    ncp = s // CMP_STRIDE
    nc = (s - CMP_LEN) // CMP_STRIDE + 1
    ns = s // SEL_BLOCK
    c_start = np.arange(ncp) * CMP_STRIDE
    s_start = np.arange(ns) * SEL_BLOCK
    ov = ((c_start[None, :] <= s_start[:, None] + SEL_BLOCK - 1) &
          (c_start[None, :] + CMP_LEN - 1 >= s_start[:, None]) & (np.arange(ncp)[None, :] < nc))
    return jnp.asarray(ov.astype(np.float32), dtype=BF16)


def _bias_tables(s):
    tq = ATT_TQ
    ncp = s // CMP_STRIDE
    slopes = jnp.asarray(_alibi_slopes(NSA_HEADS)).reshape(NSA_KV_HEADS, NSA_GROUP)
    srow = jnp.repeat(slopes, tq, axis=1)[:, None, :] * LOG2E
    t_rel = jnp.tile(jnp.arange(tq, dtype=jnp.int32), NSA_GROUP)[None, None, :]

    def table(rel_pos, lo, hi):
        dist = t_rel - rel_pos[None, :, None]
        bias = -srow * dist.astype(F32)
        return jnp.where((dist >= lo) & (dist < hi), bias, NEG)

    far = 1 << 30
    crel = (jnp.arange(2 * ncp, dtype=jnp.int32) - ncp) * CMP_STRIDE + (CMP_LEN - 1)
    t_c = table(crel, 0, far)
    t_w = table(jnp.arange(WINDOW + tq, dtype=jnp.int32) - WINDOW, 0, WINDOW)
    blk = jnp.arange(SUPER, dtype=jnp.int32)
    t_s = jnp.stack([table(blk, -far, far), table(blk, 0, far)], axis=1)
    return t_c, t_w, t_s, srow


def _nsa(qT, kvs, ng, cmp_pos, w_ck1, w_ck2, w_cv1, w_cv2):
    b, s, _ = kvs.shape
    kv, grp, dh = NSA_KV_HEADS, NSA_GROUP, HEAD_DIM
    tq = ATT_TQ
    nqt = s // tq
    ns = s // SEL_BLOCK
    kc, vc, ks, vs, kw, vw = [kvs[..., i * KV_WIDTH:(i + 1) * KV_WIDTH] for i in range(6)]
    ckf = _compress(kc, cmp_pos, w_ck1, w_ck2)
    cvf = _compress(vc, cmp_pos, w_cv1, w_cv2)
    heads = lambda v: v.reshape(b, -1, kv, dh)
    rows = lambda v: heads(v).transpose(0, 2, 1, 3)
    cols = lambda v: heads(v).transpose(0, 2, 3, 1)
    tiles = lambda v, c: v.reshape(b, nqt, tq, kv, grp, c).transpose(0, 3, 1, 5, 4, 2).reshape(
        b, kv, nqt, c, grp * tq)
    gatesT = tiles(ng, 3)
    t_c, t_w, t_s, srow = _bias_tables(s)

    part, imp = _attn1(qT, rows(ckf), cols(cvf), rows(kw), cols(vw), gatesT, t_c, t_w, _overlap_T(s))
    imp_t = imp.reshape(b, kv, nqt // TOPK_TILES, TOPK_TILES, ns, tq).transpose(0, 1, 2, 4, 3, 5)
    sel_t, lst = _topk(imp_t)
    return _attn2(lst, qT, rows(ks), cols(vs), sel_t, gatesT, t_s, srow, part)


HALO = 8
CONV_SLAB = 512


def _softplus(x):
    return jnp.maximum(x, 0.0) + jnp.log1p(jnp.exp(-jnp.abs(x)))


def _ssd_kernel(raw_ref, z_ref, dt_ref, dtT_ref, cw_ref, cb_ref, dtb_ref, dtbT_ref,
                al_ref, alT_ref, dsk_ref, nw_ref, o_ref, st_ref, tail_ref, ext_ref, xc_ref, y_ref):
    q = SSM_CHUNK

    @pl.when(pl.program_id(1) == 0)
    def _():
        st_ref[...] = jnp.zeros(st_ref.shape, F32)
        tail_ref[...] = jnp.zeros(tail_ref.shape, F32)

    ext_ref[0:HALO, :] = tail_ref[...]
    ext_ref[HALO:HALO + q, :] = raw_ref[0].astype(F32)
    tail_ref[...] = ext_ref[q:q + HALO, :]
    for c0 in range(0, CONV_DIM, CONV_SLAB):
        sl = slice(c0, c0 + CONV_SLAB)
        y = cb_ref[:, sl]
        for k in range(CONV_WIDTH):
            off = HALO - (CONV_WIDTH - 1) + k
            y = y + cw_ref[k:k + 1, sl] * ext_ref[off:off + q, sl]
        xc_ref[:, sl] = y * jax.nn.sigmoid(y)

    dt = _softplus(dt_ref[0][:, :SSM_HEADS] + dtb_ref[...])
    dtT = _softplus(dtT_ref[0] + dtbT_ref[...])
    ri = lax.broadcasted_iota(jnp.int32, (q, q), 0)
    ci = lax.broadcasted_iota(jnp.int32, (q, q), 1)
    tri = ri >= ci
    low = jnp.where(tri, 1.0, 0.0).astype(BF16)
    upp = jnp.where(ri <= ci, 1.0, 0.0).astype(BF16)
    cum = _dot01_l(low, dt * (-jnp.exp(al_ref[...])))
    cumT = _dot01_r(dtT * (-jnp.exp(alT_ref[...])), upp)
    ecum = jnp.exp(cum)
    lane_lo = lax.broadcasted_iota(jnp.int32, (1, LANES), 1) < SSM_HEAD_DIM
    b0 = SSM_INNER
    c0 = SSM_INNER + SSM_GROUPS * SSM_STATE

    for g in range(SSM_GROUPS):
        bgf = xc_ref[:, b0 + g * SSM_STATE:b0 + (g + 1) * SSM_STATE]
        cg = xc_ref[:, c0 + g * SSM_STATE:c0 + (g + 1) * SSM_STATE].astype(BF16)
        cb = lax.dot_general(cg, bgf.astype(BF16), (((1,), (1,)), ((), ())), preferred_element_type=F32)
        bTg = bgf.T
        for pr in range(SSM_HPG // 2):
            hp = g * (SSM_HPG // 2) + pr
            xp = xc_ref[:, hp * LANES:(hp + 1) * LANES].astype(BF16)
            ys, sts, ecs, els = [], [], [], []
            for k in range(2):
                h = 2 * hp + k
                crow = cumT[h:h + 1, :]
                seg = cum[:, h:h + 1] - crow
                w = cb * jnp.exp(jnp.where(tri, seg, NEG)) * dtT[h:h + 1, :]
                ys.append(jnp.dot(w.astype(BF16), xp, preferred_element_type=F32))
                clast = crow[:, q - 1:q]
                to_end = jnp.exp(clast - crow) * dtT[h:h + 1, :]
                sts.append(jnp.dot((bTg * to_end).astype(BF16), xp, preferred_element_type=F32))
                ecs.append(ecum[:, h:h + 1])
                els.append(jnp.exp(clast))
            st = st_ref[hp]
            y_in = jnp.dot(cg, st.astype(BF16), preferred_element_type=F32)
            y_ref[:, hp * LANES:(hp + 1) * LANES] = (
                jnp.where(lane_lo, ys[0], ys[1]) + y_in * jnp.where(lane_lo, ecs[0], ecs[1]))
            st_ref[hp] = st * jnp.where(lane_lo, els[0], els[1]) + jnp.where(lane_lo, sts[0], sts[1])

    gw = SSM_INNER // SSM_GROUPS
    for g in range(SSM_GROUPS):
        sl = slice(g * gw, (g + 1) * gw)
        z = z_ref[0][:, sl].astype(F32)
        y = (y_ref[:, sl] + dsk_ref[:, sl] * xc_ref[:, sl]) * (z * jax.nn.sigmoid(z))
        y = y * lax.rsqrt(jnp.mean(y * y, axis=-1, keepdims=True) + EPS)
        o_ref[0, :, sl] = (y * nw_ref[:, sl]).astype(o_ref.dtype)


def _ssd(xbc, z, dts, conv_w, conv_b, dt_bias, a_log, d_skip, ssm_norm):
    b, s, _ = xbc.shape
    q = SSM_CHUNK
    hh = SSM_HEADS
    dtT = dts[..., :hh].transpose(0, 2, 1)
    row = lambda v: v.reshape(1, -1).astype(F32)
    col = lambda v: v.reshape(-1, 1).astype(F32)
    const = lambda shape: pl.BlockSpec(shape, lambda bi, c: (0, 0))
    return pl.pallas_call(
        _ssd_kernel,
        grid=(b, s // q),
        in_specs=[pl.BlockSpec((1, q, CONV_DIM), lambda bi, c: (bi, c, 0)),
                  pl.BlockSpec((1, q, SSM_INNER), lambda bi, c: (bi, c, 0)),
                  pl.BlockSpec((1, q, LANES), lambda bi, c: (bi, c, 0)),
                  pl.BlockSpec((1, hh, q), lambda bi, c: (bi, 0, c)),
                  const((CONV_WIDTH, CONV_DIM)), const((1, CONV_DIM)),
                  const((1, hh)), const((hh, 1)), const((1, hh)), const((hh, 1)),
                  const((1, SSM_INNER)), const((1, SSM_INNER))],
        out_specs=pl.BlockSpec((1, q, SSM_INNER), lambda bi, c: (bi, c, 0)),
        out_shape=jax.ShapeDtypeStruct((b, s, SSM_INNER), BF16),
        scratch_shapes=[pltpu.VMEM((hh // 2, SSM_STATE, LANES), F32),
                        pltpu.VMEM((HALO, CONV_DIM), F32),
                        pltpu.VMEM((q + HALO, CONV_DIM), F32),
                        pltpu.VMEM((q, CONV_DIM), F32),
                        pltpu.VMEM((q, SSM_INNER), F32)],
        compiler_params=_cparams(("parallel", "arbitrary")),
        name="ssd",
    )(xbc, z, dts, dtT, conv_w, row(conv_b), row(dt_bias), col(dt_bias), row(a_log), col(a_log),
      row(jnp.repeat(d_skip, SSM_HEAD_DIM)), row(ssm_norm))


MERGE_TM = 512


def _merge_kernel(h_ref, y_ref, ym_ref, mg_ref, waT_ref, ws_ref, wo_ref, post_ref, o_ref):
    d = h_ref.shape[1]
    tq = y_ref.shape[4] // NSA_GROUP
    yT = jnp.concatenate(
        [jnp.concatenate([y_ref[0, kv, t, :, h * tq:(h + 1) * tq]
                          for kv in range(NSA_KV_HEADS) for h in range(NSA_GROUP)], axis=0)
         for t in range(y_ref.shape[2])], axis=1)
    a = jnp.dot(waT_ref[...], yT, preferred_element_type=F32).T
    m = jnp.dot(ym_ref[...], ws_ref[...], preferred_element_type=F32)
    gts = jax.nn.sigmoid(mg_ref[...].astype(F32))
    merged = gts[:, :d] * a + gts[:, d:] * m
    out = jnp.dot(merged.astype(BF16), wo_ref[...], preferred_element_type=F32)
    o_ref[...] = h_ref[...] + _rms(out, post_ref[...])


def _merge(h2d, ya_tiles, ym, mg, wa, ws, wo, post):
    t, d = h2d.shape
    b, kv, nqt, dh, n = ya_tiles.shape
    tq = n // NSA_GROUP
    s = nqt * tq
    tm = min(MERGE_TM, s)
    per_seq = s // tm
    const = lambda shape: pl.BlockSpec(shape, lambda i: (0, 0), pipeline_mode=pl.Buffered(1))
    rows = lambda c: pl.BlockSpec((tm, c), lambda i: (i, 0))
    return pl.pallas_call(
        _merge_kernel,
        grid=(t // tm,),
        in_specs=[rows(d),
                  pl.BlockSpec((1, kv, tm // tq, dh, n), lambda i: (i // per_seq, 0, i % per_seq, 0, 0)),
                  rows(SSM_INNER), rows(2 * d),
                  const((d, NSA_WIDTH)), const((SSM_INNER, d)), const((d, d)), const((1, d))],
        out_specs=rows(d),
        out_shape=jax.ShapeDtypeStruct((t, d), F32),
        compiler_params=_cparams(("parallel",)),
        name="merge",
    )(h2d, ya_tiles, ym, mg, wa.T.astype(BF16), ws.astype(BF16), wo.astype(BF16), post.reshape(1, d))


def _mixer(h2d, b, s, mix_pre, w_in, cmp_pos, w_ck1, w_ck2, w_cv1, w_cv2, conv_w, conv_b, dt_bias,
           a_log, d_skip, ssm_norm, w_attn_branch, w_ssm_branch, w_out, mix_post):
    o = 0
    cols = {}
    for name, width in (("q", NSA_WIDTH), ("kv", 6 * KV_WIDTH), ("ng", NSA_HEADS * 3), ("z", SSM_INNER),
                        ("xbc", CONV_DIM), ("dt", SSM_HEADS), ("mg", 2 * D_MODEL)):
        cols[name] = w_in[:, o:o + width]
        o += width
    pad = LANES - SSM_HEADS - NSA_HEADS * 3
    w_small = jnp.concatenate([cols["dt"], cols["ng"], jnp.zeros((D_MODEL, pad), F32)], axis=1)

    u = _rmsnorm(h2d, mix_pre)
    qT = _proj_q(u, cols["q"], b, s, HEAD_DIM ** -0.5 * LOG2E, ATT_TQ)
    kvs = _proj(u, cols["kv"], BF16).reshape(b, s, 6 * KV_WIDTH)
    z = _proj(u, cols["z"], BF16).reshape(b, s, SSM_INNER)
    xbc = _proj(u, cols["xbc"], BF16).reshape(b, s, CONV_DIM)
    mg = _proj(u, cols["mg"], BF16)
    small = _proj(u, w_small, F32).reshape(b, s, LANES)
    ng = small[..., SSM_HEADS:SSM_HEADS + NSA_HEADS * 3]

    y_a = _nsa(qT, kvs, ng, cmp_pos, w_ck1, w_ck2, w_cv1, w_cv2)
    y_m = _ssd(xbc, z, small, conv_w, conv_b, dt_bias, a_log, d_skip, ssm_norm)
    return _merge(h2d, y_a, y_m.reshape(b * s, SSM_INNER), mg,
                  w_attn_branch, w_ssm_branch, w_out, mix_post)


def kernel(x, ffn1_pre, ffn1_gate, ffn1_up, ffn1_down, ffn1_post, mix_pre, w_in, cmp_pos, w_ck1, w_ck2,
           w_cv1, w_cv2, conv_w, conv_b, dt_bias, a_log, d_skip, ssm_norm, w_attn_branch, w_ssm_branch,
           w_out, mix_post, ffn2_pre, ffn2_gate, ffn2_up, ffn2_down, ffn2_post):
    b, s, d = x.shape
    h = x.reshape(b * s, d)
    for l in range(ffn1_pre.shape[0]):
        h = _ffn(h, ffn1_pre[l], ffn1_gate[l], ffn1_up[l], ffn1_down[l], ffn1_post[l])
        h = _mixer(h, b, s, mix_pre[l], w_in[l], cmp_pos[l], w_ck1[l], w_ck2[l], w_cv1[l], w_cv2[l],
                   conv_w[l], conv_b[l], dt_bias[l], a_log[l], d_skip[l], ssm_norm[l],
                   w_attn_branch[l], w_ssm_branch[l], w_out[l], mix_post[l])
        h = _ffn(h, ffn2_pre[l], ffn2_gate[l], ffn2_up[l], ffn2_down[l], ffn2_post[l])
    return h.reshape(b, s, d)
```
